```python
import jax, jax.numpy as jnp
from jax import lax
import numpy as np

D_MODEL = 4096
BATCH = 2
SEQ = 8192
DEPTH = 4

RET_HEADS = 8
RET_DK = 128
RET_DV = 256
RET_QK = RET_HEADS * RET_DK
RET_V = RET_HEADS * RET_DV
RET_CHUNK = 128
ROPE_BASE = 10000.0
LRU_WIDTH = D_MODEL // 2
LRU_BLOCKS = 16
LRU_BW = LRU_WIDTH // LRU_BLOCKS
CONV_WIDTH = 4
LRU_C = 8.0
MIX_WIDTH = RET_V + LRU_WIDTH
IN_WIDTH = 2 * RET_QK + 2 * RET_V + 2 * LRU_WIDTH
SPLITS = (RET_QK, 2 * RET_QK, 2 * RET_QK + RET_V, 2 * RET_QK + 2 * RET_V, 2 * RET_QK + 2 * RET_V + LRU_WIDTH)
D_FF = 2 * D_MODEL
N_EXPERTS = 8
TOP_K = 2
D_EXPERT = 7 * D_MODEL // 16
N_DENSE = (DEPTH + 1) // 2
N_MOE = DEPTH // 2
N_MOD = 6
EPS = 1e-6

kernel_name = "hybrid_retention_rglru_moe_adaln"


def rms_norm(x):
    xf = x.astype(jnp.float32)
    return (xf * lax.rsqrt(jnp.mean(xf * xf, axis=-1, keepdims=True) + EPS)).astype(x.dtype)


def rope_tables(positions):
    inv = jnp.exp2(-jnp.arange(0, RET_DK, 2, dtype=jnp.float32) / RET_DK * jnp.log2(jnp.float32(ROPE_BASE)))
    ang = positions.astype(jnp.float32)[..., None] * inv
    return jnp.cos(ang)[:, :, None, :], jnp.sin(ang)[:, :, None, :]


def apply_rope(t, cos, sin):
    half = t.shape[-1] // 2
    t1, t2 = t[..., :half], t[..., half:]
    return jnp.concatenate([t1 * cos - t2 * sin, t1 * sin + t2 * cos], axis=-1)


def retention(q, k, v):
    b, s, h, dk = q.shape
    dv = v.shape[-1]
    n = s // RET_CHUNK
    C = RET_CHUNK
    log_gamma = jnp.log1p(-jnp.exp2(-5.0 - jnp.arange(h, dtype=jnp.float32)))

    def chunks(t):
        return t.astype(jnp.float32).reshape(b, n, C, h, t.shape[-1]).transpose(1, 0, 3, 2, 4)

    qc, kc, vc = chunks(q), chunks(k * (dk ** -0.5)), chunks(v)
    idx = jnp.arange(C, dtype=jnp.float32)
    rel = idx[:, None] - idx[None, :]
    decay = jnp.where(rel >= 0, jnp.exp(log_gamma[:, None, None] * jnp.maximum(rel, 0.0)), 0.0)
    q_decay = jnp.exp(log_gamma[:, None] * (idx[None, :] + 1.0))[None, :, :, None]
    k_decay = jnp.exp(log_gamma[:, None] * (C - 1.0 - idx[None, :]))[None, :, :, None]
    chunk_decay = jnp.exp(log_gamma * C)[None, :, None, None]

    def step(state, xs):
        qi, ki, vi = xs
        scores = jnp.einsum('bhqd,bhkd->bhqk', qi, ki) * decay
        intra = jnp.einsum('bhqk,bhkv->bhqv', scores, vi)
        cross = jnp.einsum('bhqd,bhdv->bhqv', qi, state) * q_decay
        new_state = state * chunk_decay + jnp.einsum('bhkd,bhkv->bhdv', ki * k_decay, vi)
        return new_state, intra + cross

    state0 = jnp.zeros((b, h, dk, dv), jnp.float32)
    _, out = lax.scan(step, state0, (qc, kc, vc))
    return out.transpose(1, 0, 3, 2, 4).reshape(b, s, h, dv)


def head_group_norm(y, gain):
    b, s, h, dv = y.shape
    mu = jnp.mean(y, axis=-1, keepdims=True)
    yc = y - mu
    yn = yc * lax.rsqrt(jnp.mean(yc * yc, axis=-1, keepdims=True) + EPS)
    return yn.reshape(b, s, h * dv) * gain


def causal_conv(x, w, bias):
    s = x.shape[1]
    xp = jnp.pad(x, ((0, 0), (CONV_WIDTH - 1, 0), (0, 0)))
    y = xp[:, 0:s] * w[0]
    for j in range(1, CONV_WIDTH):
        y = y + xp[:, j:j + s] * w[j]
    return y + bias


def rg_lru(x, w_a, b_a, w_x, b_x, lam):
    b, s, w = x.shape
    xf = x.astype(jnp.float32)
    xb = xf.reshape(b, s, LRU_BLOCKS, LRU_BW)
    r = jax.nn.sigmoid(jnp.einsum('bsnc,ncd->bsnd', xb, w_a.astype(jnp.float32)).reshape(b, s, w) + b_a)
    i = jax.nn.sigmoid(jnp.einsum('bsnc,ncd->bsnd', xb, w_x.astype(jnp.float32)).reshape(b, s, w) + b_x)
    log_a = -LRU_C * r * jax.nn.softplus(-lam.astype(jnp.float32))
    a = jnp.exp(log_a)
    u = jnp.sqrt(-jnp.expm1(2.0 * log_a)) * (i * xf)

    def combine(left, right):
        a1, b1 = left
        a2, b2 = right
        return a1 * a2, a2 * b1 + b2

    _, h = lax.associative_scan(combine, (a, u), axis=1)
    return h


def hybrid_mixer(hn, cos, sin, w_in, conv_w, conv_b, gate_a_w, gate_a_b, gate_x_w, gate_x_b,
                 lru_lambda, ret_norm_g, lru_norm_g, w_out):
    b, s, _ = hn.shape
    proj = hn @ w_in
    q, k, v, g, lru_gate, lru_x = jnp.split(proj, SPLITS, axis=-1)
    q = apply_rope(q.reshape(b, s, RET_HEADS, RET_DK), cos, sin)
    k = apply_rope(k.reshape(b, s, RET_HEADS, RET_DK), cos, sin)
    v = v.reshape(b, s, RET_HEADS, RET_DV)
    y_ret = head_group_norm(retention(q, k, v), ret_norm_g.astype(jnp.float32))
    y_ret = (jax.nn.silu(g.astype(jnp.float32)) * y_ret).astype(hn.dtype)
    xc = causal_conv(lru_x, conv_w, conv_b)
    h = rg_lru(xc, gate_a_w, gate_a_b, gate_x_w, gate_x_b, lru_lambda)
    h = h * lax.rsqrt(jnp.mean(h * h, axis=-1, keepdims=True) + EPS) * lru_norm_g.astype(jnp.float32)
    y_lru = (jax.nn.gelu(lru_gate.astype(jnp.float32)) * h).astype(hn.dtype)
    return jnp.concatenate([y_ret, y_lru], axis=-1) @ w_out


def swiglu(h, w_gate, w_up, w_down):
    return (jax.nn.silu(h @ w_gate) * (h @ w_up)) @ w_down


def moe_swiglu(h, w_router, w_gate, w_up, w_down):
    b, s, d = h.shape
    t = h.reshape(b * s, d)
    logits = (t @ w_router).astype(jnp.float32)
    top_val, top_idx = lax.top_k(logits, TOP_K)
    top_w = jax.nn.softmax(top_val, axis=-1)
    comb = jnp.sum(jax.nn.one_hot(top_idx, N_EXPERTS, dtype=jnp.float32) * top_w[..., None], axis=1)
    out = jnp.zeros((b * s, d), jnp.float32)
    for e in range(N_EXPERTS):
        he = jax.nn.silu(t @ w_gate[e]) * (t @ w_up[e])
        out = out + comb[:, e:e + 1] * (he @ w_down[e])
    return out.reshape(b, s, d).astype(h.dtype)


def setup_inputs(seed: int = 0) -> dict:
    key = jax.random.key(seed)
    ks = jax.random.split(key, 32)
    f32 = jnp.float32

    def nrm(k, shape, scale):
        return jax.random.normal(k, shape, f32) * scale

    x = nrm(ks[0], (BATCH, SEQ, D_MODEL), 1.0)
    c = nrm(ks[1], (BATCH, D_MODEL), 1.0)
    offsets = jax.random.randint(ks[2], (BATCH, 1), 0, 1024, dtype=jnp.int32)
    positions = offsets + jnp.arange(SEQ, dtype=jnp.int32)[None, :]
    ada_w = nrm(ks[3], (D_MODEL, N_MOD * D_MODEL), 0.5 * D_MODEL ** -0.5)
    ada_b = nrm(ks[4], (N_MOD * D_MODEL,), 0.02)
    ada_table = nrm(ks[5], (DEPTH, N_MOD, D_MODEL), 0.1)
    w_in = nrm(ks[6], (DEPTH, D_MODEL, IN_WIDTH), D_MODEL ** -0.5)
    conv_w = nrm(ks[7], (DEPTH, CONV_WIDTH, LRU_WIDTH), CONV_WIDTH ** -0.5)
    conv_b = nrm(ks[8], (DEPTH, LRU_WIDTH), 0.01)
    gate_a_w = nrm(ks[9], (DEPTH, LRU_BLOCKS, LRU_BW, LRU_BW), LRU_BW ** -0.5)
    gate_a_b = nrm(ks[10], (DEPTH, LRU_WIDTH), 0.01)
    gate_x_w = nrm(ks[11], (DEPTH, LRU_BLOCKS, LRU_BW, LRU_BW), LRU_BW ** -0.5)
    gate_x_b = nrm(ks[12], (DEPTH, LRU_WIDTH), 0.01)
    a_c = jax.random.uniform(ks[13], (DEPTH, LRU_WIDTH), f32, 0.9, 0.999)
    a0 = a_c ** (1.0 / LRU_C)
    lru_lambda = jnp.log(a0) - jnp.log1p(-a0)
    ret_norm_g = 1.0 + nrm(ks[14], (DEPTH, RET_V), 0.02)
    lru_norm_g = 1.0 + nrm(ks[15], (DEPTH, LRU_WIDTH), 0.02)
    w_out = nrm(ks[16], (DEPTH, MIX_WIDTH, D_MODEL), MIX_WIDTH ** -0.5)
    ffn_w_gate = nrm(ks[17], (N_DENSE, D_MODEL, D_FF), D_MODEL ** -0.5)
    ffn_w_up = nrm(ks[18], (N_DENSE, D_MODEL, D_FF), D_MODEL ** -0.5)
    ffn_w_down = nrm(ks[19], (N_DENSE, D_FF, D_MODEL), D_FF ** -0.5)
    router_w = nrm(ks[20], (N_MOE, D_MODEL, N_EXPERTS), D_MODEL ** -0.5)
    moe_w_gate = nrm(ks[21], (N_MOE, N_EXPERTS, D_MODEL, D_EXPERT), D_MODEL ** -0.5)
    moe_w_up = nrm(ks[22], (N_MOE, N_EXPERTS, D_MODEL, D_EXPERT), D_MODEL ** -0.5)
    moe_w_down = nrm(ks[23], (N_MOE, N_EXPERTS, D_EXPERT, D_MODEL), D_EXPERT ** -0.5)
    final_norm_g = 1.0 + nrm(ks[24], (D_MODEL,), 0.02)
    return {"x": x, "c": c, "positions": positions, "ada_w": ada_w, "ada_b": ada_b,
            "ada_table": ada_table, "w_in": w_in, "conv_w": conv_w, "conv_b": conv_b,
            "gate_a_w": gate_a_w, "gate_a_b": gate_a_b, "gate_x_w": gate_x_w, "gate_x_b": gate_x_b,
            "lru_lambda": lru_lambda, "ret_norm_g": ret_norm_g, "lru_norm_g": lru_norm_g,
            "w_out": w_out, "ffn_w_gate": ffn_w_gate, "ffn_w_up": ffn_w_up, "ffn_w_down": ffn_w_down,
            "router_w": router_w, "moe_w_gate": moe_w_gate, "moe_w_up": moe_w_up,
            "moe_w_down": moe_w_down, "final_norm_g": final_norm_g}


def reference(x, c, positions, ada_w, ada_b, ada_table, w_in, conv_w, conv_b, gate_a_w, gate_a_b,
              gate_x_w, gate_x_b, lru_lambda, ret_norm_g, lru_norm_g, w_out, ffn_w_gate, ffn_w_up,
              ffn_w_down, router_w, moe_w_gate, moe_w_up, moe_w_down, final_norm_g):
    b = x.shape[0]
    mod_base = (jax.nn.silu(c) @ ada_w + ada_b).reshape(b, N_MOD, D_MODEL)
    cos, sin = rope_tables(positions)
    for l in range(DEPTH):
        mod = mod_base + ada_table[l]
        shift1, scale1, gate1, shift2, scale2, gate2 = [mod[:, j, None, :] for j in range(N_MOD)]
        hn = rms_norm(x) * (1.0 + scale1) + shift1
        y = hybrid_mixer(hn, cos, sin, w_in[l], conv_w[l], conv_b[l], gate_a_w[l], gate_a_b[l],
                         gate_x_w[l], gate_x_b[l], lru_lambda[l], ret_norm_g[l], lru_norm_g[l], w_out[l])
        x = x + gate1 * y
        hn = rms_norm(x) * (1.0 + scale2) + shift2
        if l % 2 == 0:
            y = swiglu(hn, ffn_w_gate[l // 2], ffn_w_up[l // 2], ffn_w_down[l // 2])
        else:
            y = moe_swiglu(hn, router_w[l // 2], moe_w_gate[l // 2], moe_w_up[l // 2], moe_w_down[l // 2])
        x = x + gate2 * y
    return rms_norm(x) * final_norm_g
```

```python
import functools

import jax
import jax.numpy as jnp
from jax import lax
from jax.experimental import pallas as pl
from jax.experimental.pallas import tpu as pltpu

F32 = jnp.float32
BF16 = jnp.bfloat16

RET_HEADS = 8
RET_CHUNK = 128
ROPE_BASE = 10000.0
LRU_C = 8.0
N_MOD = 6
EPS = 1e-6

VMEM_LIMIT_BYTES = 56 * 1024 * 1024
LANES = 128

TM_MM = 1024
TN_MM = 1024
TN_RES = 512
TM_DOWN = 512
TN_FFN = 512
TS_NORM = 256
R_RET = 512
R_LRU = 256
TM_MOE = 512
TN_MOE_UP = 256
TN_MOE_DOWN = 1024
G_ROWS = 128
TN_ADA = 512


def _tile(dim, pref):
    if dim <= pref:
        return dim
    for t in range(pref, 0, -1):
        if dim % t == 0 and t % LANES == 0:
            return t
    for t in range(pref, 0, -1):
        if dim % t == 0 and t % 8 == 0:
            return t
    return dim


def _params(*sem):
    return pltpu.CompilerParams(dimension_semantics=sem, vmem_limit_bytes=VMEM_LIMIT_BYTES)


def _silu(x):
    return x * jax.nn.sigmoid(x)


def _dot(a, b):
    return jnp.dot(a, b, preferred_element_type=F32)


def _ada_kernel(c_ref, w_ref, b_ref, o_ref):
    c = c_ref[...]
    o_ref[...] = _dot(_silu(c).astype(BF16), w_ref[...].astype(BF16)) + b_ref[...]


def _ada_base(c, ada_w, ada_b):
    b, d = c.shape
    n = ada_w.shape[1]
    rows = 8
    cp = jnp.zeros((rows, d), F32).at[:b].set(c)
    tn = _tile(n, TN_ADA)
    out = pl.pallas_call(
        _ada_kernel,
        grid=(n // tn,),
        in_specs=[pl.BlockSpec((rows, d), lambda j: (0, 0)),
                  pl.BlockSpec((d, tn), lambda j: (0, j)),
                  pl.BlockSpec((1, tn), lambda j: (0, j))],
        out_specs=pl.BlockSpec((rows, tn), lambda j: (0, j)),
        out_shape=jax.ShapeDtypeStruct((rows, n), F32),
        compiler_params=_params("arbitrary"),
        name="ada_base",
    )(cp, ada_w, ada_b.reshape(1, n))
    return out[:b].reshape(b, N_MOD, d)


def _rope_kernel(pos_ref, inv_ref, sign_ref, cos_ref, sin_ref):
    ang = pos_ref[...].astype(F32) * inv_ref[...]
    cos_ref[...] = jnp.cos(ang)
    sin_ref[...] = jnp.sin(ang) * sign_ref[...]


def _rope_tables(positions, dk):
    b, s = positions.shape
    inv = jnp.exp2(-jnp.arange(0, dk, 2, dtype=F32) / dk * jnp.log2(jnp.float32(ROPE_BASE)))
    inv2 = jnp.concatenate([inv, inv]).reshape(1, dk)
    sign = jnp.concatenate([-jnp.ones((dk // 2,), F32), jnp.ones((dk // 2,), F32)]).reshape(1, dk)
    ts = _tile(s, 1024)
    spec = pl.BlockSpec((None, ts, dk), lambda i, j: (i, j, 0))
    return pl.pallas_call(
        _rope_kernel,
        grid=(b, s // ts),
        in_specs=[pl.BlockSpec((None, ts, 1), lambda i, j: (i, j, 0)),
                  pl.BlockSpec((1, dk), lambda i, j: (0, 0)),
                  pl.BlockSpec((1, dk), lambda i, j: (0, 0))],
        out_specs=[spec, spec],
        out_shape=[jax.ShapeDtypeStruct((b, s, dk), F32)] * 2,
        compiler_params=_params("arbitrary", "arbitrary"),
        name="rope_tables",
    )(positions.reshape(b, s, 1), inv2, sign)


def _mod_rows(mb_ref, tb_ref, j):
    return mb_ref[j:j + 1, :] + tb_ref[j:j + 1, :]


def _normed(x):
    return x * lax.rsqrt(jnp.mean(x * x, axis=-1, keepdims=True) + EPS)


def _norm_mod_kernel(x_ref, mb_ref, tb_ref, o_ref, *, j_shift):
    xn = _normed(x_ref[...])
    shift = _mod_rows(mb_ref, tb_ref, j_shift)
    scale = _mod_rows(mb_ref, tb_ref, j_shift + 1)
    o_ref[...] = (xn * (1.0 + scale) + shift).astype(o_ref.dtype)


def _norm_mod(x, mod_base, table, j_shift):
    b, s, d = x.shape
    ts = _tile(s, TS_NORM)
    return pl.pallas_call(
        functools.partial(_norm_mod_kernel, j_shift=j_shift),
        grid=(b, s // ts),
        in_specs=[pl.BlockSpec((None, ts, d), lambda i, j: (i, j, 0)),
                  pl.BlockSpec((None, N_MOD, d), lambda i, j: (i, 0, 0)),
                  pl.BlockSpec((N_MOD, d), lambda i, j: (0, 0))],
        out_specs=pl.BlockSpec((None, ts, d), lambda i, j: (i, j, 0)),
        out_shape=jax.ShapeDtypeStruct((b, s, d), BF16),
        compiler_params=_params("arbitrary", "arbitrary"),
        name="norm_mod",
    )(x, mod_base, table)


def _final_norm_kernel(x_ref, g_ref, o_ref):
    o_ref[...] = _normed(x_ref[...]) * g_ref[...]


def _final_norm(x, gain):
    b, s, d = x.shape
    ts = _tile(s, TS_NORM)
    return pl.pallas_call(
        _final_norm_kernel,
        grid=(b, s // ts),
        in_specs=[pl.BlockSpec((None, ts, d), lambda i, j: (i, j, 0)),
                  pl.BlockSpec((1, d), lambda i, j: (0, 0))],
        out_specs=pl.BlockSpec((None, ts, d), lambda i, j: (i, j, 0)),
        out_shape=jax.ShapeDtypeStruct((b, s, d), F32),
        compiler_params=_params("arbitrary", "arbitrary"),
        name="final_norm",
    )(x, gain.reshape(1, d))


def _router_kernel(x_ref, mb_ref, tb_ref, rw_ref, hn_ref, comb_ref, mask_ref, *, j_shift, n_experts):
    xn = _normed(x_ref[...])
    shift = _mod_rows(mb_ref, tb_ref, j_shift)
    scale = _mod_rows(mb_ref, tb_ref, j_shift + 1)
    hn = xn * (1.0 + scale) + shift
    hn_ref[...] = hn
    h_hi = hn.astype(BF16)
    h_lo = (hn - h_hi.astype(F32)).astype(BF16)
    rw = rw_ref[...]
    w_hi = rw.astype(BF16)
    w_lo = (rw - w_hi.astype(F32)).astype(BF16)
    logits = _dot(h_hi, w_hi) + (_dot(h_lo, w_hi) + _dot(h_hi, w_lo))
    lane = lax.broadcasted_iota(jnp.int32, logits.shape, 1)
    neg = jnp.float32(-jnp.inf)
    l1 = jnp.where(lane < n_experts, logits, neg)
    m1 = jnp.max(l1, axis=-1, keepdims=True)
    i1 = jnp.min(jnp.where(l1 == m1, lane, LANES), axis=-1, keepdims=True)
    sel1 = lane == i1
    l2 = jnp.where(sel1, neg, l1)
    m2 = jnp.max(l2, axis=-1, keepdims=True)
    i2 = jnp.min(jnp.where(l2 == m2, lane, LANES), axis=-1, keepdims=True)
    sel2 = lane == i2
    e2 = jnp.exp(m2 - m1)
    den = 1.0 + e2
    comb_ref[...] = jnp.where(sel1, 1.0 / den, 0.0) + jnp.where(sel2, e2 / den, 0.0)
    mask_ref[...] = jnp.where(jnp.logical_or(sel1, sel2), 1.0, 0.0)


def _router(x, mod_base, table, j_shift, router_w):
    b, s, d = x.shape
    e = router_w.shape[1]
    rw = jnp.zeros((d, LANES), F32).at[:, :e].set(router_w)
    ts = _tile(s, TS_NORM)
    row = pl.BlockSpec((None, ts, d), lambda i, j: (i, j, 0))
    lane_out = pl.BlockSpec((None, ts, LANES), lambda i, j: (i, j, 0))
    return pl.pallas_call(
        functools.partial(_router_kernel, j_shift=j_shift, n_experts=e),
        grid=(b, s // ts),
        in_specs=[row,
                  pl.BlockSpec((None, N_MOD, d), lambda i, j: (i, 0, 0)),
                  pl.BlockSpec((N_MOD, d), lambda i, j: (0, 0)),
                  pl.BlockSpec((d, LANES), lambda i, j: (0, 0))],
        out_specs=[row, lane_out, lane_out],
        out_shape=[jax.ShapeDtypeStruct((b, s, d), F32),
                   jax.ShapeDtypeStruct((b, s, LANES), F32),
                   jax.ShapeDtypeStruct((b, s, LANES), F32)],
        compiler_params=_params("arbitrary", "arbitrary"),
        name="router",
    )(x, mod_base, table, rw)


def _mm_kernel(a_ref, w_ref, o_ref):
    o_ref[...] = _dot(a_ref[...], w_ref[...]).astype(o_ref.dtype)


def _matmul(a, w_stack, layer):
    m, k = a.shape
    n = w_stack.shape[2]
    tm, tn = _tile(m, TM_MM), _tile(n, TN_MM)
    return pl.pallas_call(
        _mm_kernel,
        grid=(m // tm, n // tn),
        in_specs=[pl.BlockSpec((tm, k), lambda i, j: (i, 0)),
                  pl.BlockSpec((None, k, tn), lambda i, j: (layer, 0, j))],
        out_specs=pl.BlockSpec((tm, tn), lambda i, j: (i, j)),
        out_shape=jax.ShapeDtypeStruct((m, n), BF16),
        compiler_params=_params("arbitrary", "arbitrary"),
        name="in_proj",
    )(a, w_stack)


def _mm_res_kernel(*refs, n_lhs, j_gate):
    a_refs = refs[:n_lhs]
    w_refs = refs[n_lhs:2 * n_lhs]
    x_ref, mb_ref, tb_ref, o_ref = refs[2 * n_lhs:]
    acc = _dot(a_refs[0][...], w_refs[0][...])
    for a_ref, w_ref in zip(a_refs[1:], w_refs[1:]):
        acc = acc + _dot(a_ref[...], w_ref[...])
    o_ref[...] = x_ref[...] + _mod_rows(mb_ref, tb_ref, j_gate) * acc


def _matmul_residual(lhs, w_stacks, layer, x, mod_base, table, j_gate, seq, tm_pref, name):
    m, n = x.shape
    tm, tn = _tile(seq, tm_pref), _tile(n, TN_RES)
    in_specs = [pl.BlockSpec((tm, a.shape[1]), lambda i, j: (i, 0)) for a in lhs]
    in_specs += [pl.BlockSpec((None, w.shape[1], tn), lambda i, j: (layer, 0, j)) for w in w_stacks]
    in_specs += [pl.BlockSpec((tm, tn), lambda i, j: (i, j)),
                 pl.BlockSpec((None, N_MOD, tn), lambda i, j: ((i * tm) // seq, 0, j)),
                 pl.BlockSpec((N_MOD, tn), lambda i, j: (0, j))]
    return pl.pallas_call(
        functools.partial(_mm_res_kernel, n_lhs=len(lhs), j_gate=j_gate),
        grid=(m // tm, n // tn),
        in_specs=in_specs,
        out_specs=pl.BlockSpec((tm, tn), lambda i, j: (i, j)),
        out_shape=jax.ShapeDtypeStruct((m, n), F32),
        compiler_params=_params("arbitrary", "arbitrary"),
        name=name,
    )(*lhs, *w_stacks, x, mod_base, table)


def _swiglu_kernel(a_ref, wg_ref, wu_ref, o_ref):
    a = a_ref[...]
    g = _dot(a, wg_ref[...])
    u = _dot(a, wu_ref[...])
    o_ref[...] = (_silu(g) * u).astype(o_ref.dtype)


def _swiglu_up(a, wg_stack, wu_stack, idx):
    m, k = a.shape
    n = wg_stack.shape[2]
    tm, tn = _tile(m, TM_MM), _tile(n, TN_FFN)
    wspec = pl.BlockSpec((None, k, tn), lambda i, j: (idx, 0, j))
    return pl.pallas_call(
        _swiglu_kernel,
        grid=(m // tm, n // tn),
        in_specs=[pl.BlockSpec((tm, k), lambda i, j: (i, 0)), wspec, wspec],
        out_specs=pl.BlockSpec((tm, tn), lambda i, j: (i, j)),
        out_shape=jax.ShapeDtypeStruct((m, n), BF16),
        compiler_params=_params("arbitrary", "arbitrary"),
        name="ffn_up",
    )(a, wg_stack, wu_stack)


def _retention_kernel(lg_ref, q_ref, k_ref, v_ref, g_ref, cos_ref, sin_ref, gain_ref, o_ref, state_ref,
                      *, n_chunks, chunk, dk):
    @pl.when(pl.program_id(2) == 0)
    def _():
        state_ref[...] = jnp.zeros_like(state_ref)

    lg = lg_ref[0:1, 0:1]
    row = lax.broadcasted_iota(jnp.int32, (chunk, chunk), 0)
    col = lax.broadcasted_iota(jnp.int32, (chunk, chunk), 1)
    rel = (row - col).astype(F32)
    decay = jnp.where(rel >= 0, jnp.exp(lg * jnp.maximum(rel, 0.0)), 0.0)
    ri = lax.broadcasted_iota(jnp.int32, (chunk, 1), 0).astype(F32)
    q_decay = jnp.exp(lg * (ri + 1.0))
    k_decay = jnp.exp(lg * (chunk - 1.0 - ri))
    chunk_decay = jnp.exp(lg * float(chunk))
    gain = gain_ref[...]
    k_scale = dk ** -0.5

    for c in range(n_chunks):
        rows = slice(c * chunk, (c + 1) * chunk)
        cos, sin = cos_ref[rows, :], sin_ref[rows, :]
        q = q_ref[rows, :].astype(F32)
        k = k_ref[rows, :].astype(F32)
        q = q * cos + pltpu.roll(q, dk // 2, axis=1) * sin
        k = (k * cos + pltpu.roll(k, dk // 2, axis=1) * sin) * k_scale
        qb = q.astype(BF16)
        vb = v_ref[rows, :]
        scores = lax.dot_general(qb, k.astype(BF16), (((1,), (1,)), ((), ())),
                                 preferred_element_type=F32) * decay
        state = state_ref[...]
        y = _dot(scores.astype(BF16), vb) + _dot(qb, state.astype(BF16)) * q_decay
        kd = (k * k_decay).astype(BF16)
        state_ref[...] = state * chunk_decay + lax.dot_general(
            kd, vb, (((0,), (0,)), ((), ())), preferred_element_type=F32)
        yc = y - jnp.mean(y, axis=-1, keepdims=True)
        yn = yc * lax.rsqrt(jnp.mean(yc * yc, axis=-1, keepdims=True) + EPS)
        g = g_ref[rows, :].astype(F32)
        o_ref[rows, :] = (_silu(g) * (yn * gain)).astype(o_ref.dtype)


def _retention(proj, cos2, sin2, gain, dk, dv):
    b, s, _ = proj.shape
    h = RET_HEADS
    r = _tile(s, R_RET)
    assert r % RET_CHUNK == 0 and (h * dk) % dv == 0
    log_gamma = jnp.log1p(-jnp.exp2(-5.0 - jnp.arange(h, dtype=F32)))
    lg = jnp.broadcast_to(log_gamma[:, None, None], (h, 8, LANES))
    k_blk, v_blk, g_blk = h, 2 * h * dk // dv, 2 * h * dk // dv + h
    return pl.pallas_call(
        functools.partial(_retention_kernel, n_chunks=r // RET_CHUNK, chunk=RET_CHUNK, dk=dk),
        grid=(b, h, s // r),
        in_specs=[pl.BlockSpec((None, 8, LANES), lambda i, j, t: (j, 0, 0)),
                  pl.BlockSpec((None, r, dk), lambda i, j, t: (i, t, j)),
                  pl.BlockSpec((None, r, dk), lambda i, j, t: (i, t, k_blk + j)),
                  pl.BlockSpec((None, r, dv), lambda i, j, t: (i, t, v_blk + j)),
                  pl.BlockSpec((None, r, dv), lambda i, j, t: (i, t, g_blk + j)),
                  pl.BlockSpec((None, r, dk), lambda i, j, t: (i, t, 0)),
                  pl.BlockSpec((None, r, dk), lambda i, j, t: (i, t, 0)),
                  pl.BlockSpec((1, dv), lambda i, j, t: (0, j))],
        out_specs=pl.BlockSpec((None, r, dv), lambda i, j, t: (i, t, j)),
        out_shape=jax.ShapeDtypeStruct((b, s, h * dv), BF16),
        scratch_shapes=[pltpu.VMEM((dk, dv), F32)],
        compiler_params=_params("arbitrary", "arbitrary", "arbitrary"),
        name="retention",
    )(lg, proj, proj, proj, proj, cos2, sin2, gain.reshape(1, h * dv))


def _gelu_tanh(x):
    return 0.5 * x * (1.0 + jnp.tanh(0.7978845608028654 * (x + 0.044715 * (x * x * x))))


def _softplus(z):
    return jnp.maximum(z, 0.0) + jnp.log1p(jnp.exp(-jnp.abs(z)))


def _scan_rows(a, u):
    n = a.shape[0]
    row = lax.broadcasted_iota(jnp.int32, a.shape, 0)
    d = 1
    while d < n:
        keep = row >= d
        a_prev = pltpu.roll(a, d, axis=0)
        u_prev = pltpu.roll(u, d, axis=0)
        u = jnp.where(keep, a * u_prev, 0.0) + u
        a = jnp.where(keep, a * a_prev, a)
        d *= 2
    return a, u


def _lru_kernel(x_ref, gate_ref, cw_ref, cb_ref, wcat_ref, ba_ref, bx_ref, lam_ref, gn_ref, o_ref,
                xext_ref, h_ref, hprev_ref, *, n_blocks, bw, conv_width, pad):
    rows = x_ref.shape[0]
    width = x_ref.shape[1]

    @pl.when(pl.program_id(1) == 0)
    def _():
        xext_ref[0:pad, :] = jnp.zeros((pad, width), F32)
        hprev_ref[...] = jnp.zeros_like(hprev_ref)

    xext_ref[pad:pad + rows, :] = x_ref[...].astype(F32)
    ss = jnp.zeros((rows, 1), F32)
    for n in range(n_blocks):
        cols = slice(n * bw, (n + 1) * bw)
        xc = cb_ref[:, cols] + xext_ref[pad - (conv_width - 1):pad - (conv_width - 1) + rows, cols] * cw_ref[0:1, cols]
        for j in range(1, conv_width):
            off = pad - (conv_width - 1) + j
            xc = xc + xext_ref[off:off + rows, cols] * cw_ref[j:j + 1, cols]
        gates = _dot(xc.astype(BF16), wcat_ref[n])
        r = jax.nn.sigmoid(gates[:, :bw] + ba_ref[:, cols])
        i = jax.nn.sigmoid(gates[:, bw:] + bx_ref[:, cols])
        log_a = (-LRU_C) * r * _softplus(-lam_ref[:, cols])
        a = jnp.exp(log_a)
        u = jnp.sqrt(jnp.maximum(1.0 - a * a, 0.0)) * (i * xc)
        a_cum, h = _scan_rows(a, u)
        h = h + a_cum * hprev_ref[0:1, cols]
        h_ref[:, cols] = h
        hprev_ref[0:1, cols] = h[rows - 1:rows, :]
        ss = ss + jnp.sum(h * h, axis=-1, keepdims=True)
    xext_ref[0:pad, :] = xext_ref[rows:rows + pad, :]
    inv = lax.rsqrt(ss / width + EPS)
    for n in range(n_blocks):
        cols = slice(n * bw, (n + 1) * bw)
        hn = h_ref[:, cols] * inv * gn_ref[:, cols]
        o_ref[:, cols] = (_gelu_tanh(gate_ref[:, cols].astype(F32)) * hn).astype(o_ref.dtype)


def _rg_lru(proj, gate_off, x_off, conv_w, conv_b, wcat, b_a, b_x, lam, gnorm):
    b, s, _ = proj.shape
    n_blocks, bw, _ = wcat.shape
    w = n_blocks * bw
    assert gate_off % w == 0 and x_off % w == 0
    r = _tile(s, R_LRU)
    pad = 8
    conv_width = conv_w.shape[0]
    vec = pl.BlockSpec((1, w), lambda i, t: (0, 0))
    return pl.pallas_call(
        functools.partial(_lru_kernel, n_blocks=n_blocks, bw=bw, conv_width=conv_width, pad=pad),
        grid=(b, s // r),
        in_specs=[pl.BlockSpec((None, r, w), lambda i, t: (i, t, x_off // w)),
                  pl.BlockSpec((None, r, w), lambda i, t: (i, t, gate_off // w)),
                  pl.BlockSpec((conv_width, w), lambda i, t: (0, 0)),
                  vec,
                  pl.BlockSpec((n_blocks, bw, 2 * bw), lambda i, t: (0, 0, 0)),
                  vec, vec, vec, vec],
        out_specs=pl.BlockSpec((None, r, w), lambda i, t: (i, t, 0)),
        out_shape=jax.ShapeDtypeStruct((b, s, w), BF16),
        scratch_shapes=[pltpu.VMEM((r + pad, w), F32), pltpu.VMEM((r, w), F32), pltpu.VMEM((8, w), F32)],
        compiler_params=_params("arbitrary", "arbitrary"),
        name="rg_lru",
    )(proj, proj, conv_w, conv_b.reshape(1, w), wcat, b_a.reshape(1, w), b_x.reshape(1, w),
      lam.reshape(1, w), gnorm.reshape(1, w))


def _gather_rows_kernel(idx_ref, src_ref, o_ref, sem, *, g):
    base = pl.program_id(0) * g

    def copy(r):
        return pltpu.make_async_copy(src_ref.at[pl.ds(idx_ref[base + r], 1), :], o_ref.at[pl.ds(r, 1), :], sem)

    def start(r, carry):
        copy(r).start()
        return carry

    def wait(r, carry):
        copy(r).wait()
        return carry

    lax.fori_loop(0, g, start, 0)
    lax.fori_loop(0, g, wait, 0)


def _gather_rows(src, idx):
    p = idx.shape[0]
    d = src.shape[1]
    g = _tile(p, G_ROWS)
    return pl.pallas_call(
        functools.partial(_gather_rows_kernel, g=g),
        grid_spec=pltpu.PrefetchScalarGridSpec(
            num_scalar_prefetch=1,
            grid=(p // g,),
            in_specs=[pl.BlockSpec(memory_space=pl.ANY)],
            out_specs=pl.BlockSpec((g, d), lambda i, idx_ref: (i, 0)),
            scratch_shapes=[pltpu.SemaphoreType.DMA(())]),
        out_shape=jax.ShapeDtypeStruct((p, d), src.dtype),
        compiler_params=_params("arbitrary"),
        name="moe_gather",
    )(idx, src)


def _moe_up_kernel(te_ref, nu_ref, a_ref, wg_ref, wu_ref, o_ref, abf_ref):
    used = pl.program_id(0) < nu_ref[0]

    @pl.when(jnp.logical_and(used, pl.program_id(1) == 0))
    def _():
        abf_ref[...] = a_ref[...].astype(BF16)

    @pl.when(used)
    def _():
        a = abf_ref[...]
        g = _dot(a, wg_ref[...])
        u = _dot(a, wu_ref[...])
        o_ref[...] = (_silu(g) * u).astype(o_ref.dtype)

    @pl.when(jnp.logical_not(used))
    def _():
        o_ref[...] = jnp.zeros_like(o_ref)


def _moe_up(xs, wg_stack, wu_stack, layer_idx, tile_expert, n_used, tm):
    p, d = xs.shape
    de = wg_stack.shape[3]
    tn = _tile(de, TN_MOE_UP)
    wspec = pl.BlockSpec((None, None, d, tn), lambda i, j, te, nu: (layer_idx, te[i], 0, j))
    return pl.pallas_call(
        _moe_up_kernel,
        grid_spec=pltpu.PrefetchScalarGridSpec(
            num_scalar_prefetch=2,
            grid=(p // tm, de // tn),
            in_specs=[pl.BlockSpec((tm, d), lambda i, j, te, nu: (i, 0)), wspec, wspec],
            out_specs=pl.BlockSpec((tm, tn), lambda i, j, te, nu: (i, j)),
            scratch_shapes=[pltpu.VMEM((tm, d), BF16)]),
        out_shape=jax.ShapeDtypeStruct((p, de), BF16),
        compiler_params=_params("arbitrary", "arbitrary"),
        name="moe_up",
    )(tile_expert, n_used, xs, wg_stack, wu_stack)


def _moe_down_kernel(te_ref, nu_ref, h_ref, wd_ref, rw_ref, o_ref):
    used = pl.program_id(0) < nu_ref[0]

    @pl.when(used)
    def _():
        o_ref[...] = _dot(h_ref[...], wd_ref[...]) * rw_ref[...]

    @pl.when(jnp.logical_not(used))
    def _():
        o_ref[...] = jnp.zeros_like(o_ref)


def _moe_down(h, wd_stack, layer_idx, row_w, tile_expert, n_used, tm):
    p, de = h.shape
    d = wd_stack.shape[3]
    tn = _tile(d, TN_MOE_DOWN)
    return pl.pallas_call(
        _moe_down_kernel,
        grid_spec=pltpu.PrefetchScalarGridSpec(
            num_scalar_prefetch=2,
            grid=(p // tm, d // tn),
            in_specs=[pl.BlockSpec((tm, de), lambda i, j, te, nu: (i, 0)),
                      pl.BlockSpec((None, None, de, tn), lambda i, j, te, nu: (layer_idx, te[i], 0, j)),
                      pl.BlockSpec((tm, 1), lambda i, j, te, nu: (i, 0))],
            out_specs=pl.BlockSpec((tm, tn), lambda i, j, te, nu: (i, j))),
        out_shape=jax.ShapeDtypeStruct((p, d), F32),
        compiler_params=_params("arbitrary", "arbitrary"),
        name="moe_down",
    )(tile_expert, n_used, h, wd_stack, row_w)


def _combine_kernel(pos_ref, y_ref, x_ref, mb_ref, tb_ref, o_ref, buf_ref, sem, *, g, n_tok, j_gate):
    base = pl.program_id(0) * g

    def copy(r, k):
        return pltpu.make_async_copy(y_ref.at[pl.ds(pos_ref[k * n_tok + base + r], 1), :],
                                     buf_ref.at[k, pl.ds(r, 1), :], sem)

    def start(r, carry):
        copy(r, 0).start()
        copy(r, 1).start()
        return carry

    def wait(r, carry):
        copy(r, 0).wait()
        copy(r, 1).wait()
        return carry

    lax.fori_loop(0, g, start, 0)
    lax.fori_loop(0, g, wait, 0)
    o_ref[...] = x_ref[...] + _mod_rows(mb_ref, tb_ref, j_gate) * (buf_ref[0] + buf_ref[1])


def _moe_combine(y, pos, x, mod_base, table, j_gate, seq):
    t, d = x.shape
    g = _tile(seq, G_ROWS)
    return pl.pallas_call(
        functools.partial(_combine_kernel, g=g, n_tok=t, j_gate=j_gate),
        grid_spec=pltpu.PrefetchScalarGridSpec(
            num_scalar_prefetch=1,
            grid=(t // g,),
            in_specs=[pl.BlockSpec(memory_space=pl.ANY),
                      pl.BlockSpec((g, d), lambda i, pos_ref: (i, 0)),
                      pl.BlockSpec((None, N_MOD, d), lambda i, pos_ref: ((i * g) // seq, 0, 0)),
                      pl.BlockSpec((N_MOD, d), lambda i, pos_ref: (0, 0))],
            out_specs=pl.BlockSpec((g, d), lambda i, pos_ref: (i, 0)),
            scratch_shapes=[pltpu.VMEM((2, g, d), F32), pltpu.SemaphoreType.DMA(())]),
        out_shape=jax.ShapeDtypeStruct((t, d), F32),
        compiler_params=_params("arbitrary"),
        name="moe_combine",
    )(pos, y, x, mod_base, table)


def _route_plan(mask, comb, tm):
    t, e = mask.shape
    sel = mask > 0.5
    seli = sel.astype(jnp.int32)
    counts = jnp.sum(seli, axis=0)
    padded = ((counts + tm - 1) // tm) * tm
    ends = jnp.cumsum(padded)
    starts = ends - padded
    rank = jnp.cumsum(seli, axis=0) - 1
    dest = starts[None, :] + rank
    big = jnp.int32(2 ** 30)
    pos2 = jnp.sort(jnp.where(sel, dest, big), axis=1)[:, :2]
    pos = pos2.T.reshape(-1).astype(jnp.int32)
    n_rows = 2 * t + e * tm
    n_tiles = n_rows // tm
    tile_start = jnp.arange(n_tiles, dtype=jnp.int32) * tm
    n_used = (ends[-1] // tm).astype(jnp.int32)
    te_raw = jnp.sum((tile_start[:, None] >= ends[None, :]).astype(jnp.int32), axis=1)
    last_e = jnp.sum((ends[-1] - 1 >= ends).astype(jnp.int32))
    tile_expert = jnp.where(tile_start < ends[-1], jnp.minimum(te_raw, e - 1), last_e).astype(jnp.int32)
    tok_ids = jnp.arange(t, dtype=jnp.int32)
    order = jnp.argsort(jnp.where(sel.T, tok_ids[None, :], tok_ids[None, :] + t), axis=1).astype(jnp.int32)
    rows = jnp.arange(n_rows, dtype=jnp.int32)
    row_e = tile_expert[rows // tm]
    row_r = rows - starts[row_e]
    valid = jnp.logical_and(rows < ends[-1], row_r < counts[row_e])
    src_tok = jnp.where(valid, order[row_e, jnp.clip(row_r, 0, t - 1)], 0).astype(jnp.int32)
    row_w = jnp.where(valid, comb[src_tok, row_e], 0.0).astype(F32).reshape(n_rows, 1)
    return src_tok, row_w, pos, tile_expert, n_used.reshape(1)


def kernel(x, c, positions, ada_w, ada_b, ada_table, w_in, conv_w, conv_b, gate_a_w, gate_a_b, gate_x_w,
           gate_x_b, lru_lambda, ret_norm_g, lru_norm_g, w_out, ffn_w_gate, ffn_w_up, ffn_w_down, router_w,
           moe_w_gate, moe_w_up, moe_w_down, final_norm_g):
    b, s, d = x.shape
    depth = w_in.shape[0]
    t = b * s
    ret_v = ret_norm_g.shape[1]
    lru_w = lru_norm_g.shape[1]
    ret_qk = (w_in.shape[2] - 2 * ret_v - 2 * lru_w) // 2
    dk, dv = ret_qk // RET_HEADS, ret_v // RET_HEADS
    gate_off = 2 * ret_qk + 2 * ret_v
    x_off = gate_off + lru_w
    n_experts = router_w.shape[2]
    tm_moe = _tile(t, TM_MOE)

    w_in_b = w_in.astype(BF16)
    w_out_ret = w_out[:, :ret_v].astype(BF16)
    w_out_lru = w_out[:, ret_v:].astype(BF16)
    wcat = jnp.concatenate([gate_a_w, gate_x_w], axis=-1).astype(BF16)
    ffn_g, ffn_u, ffn_d = ffn_w_gate.astype(BF16), ffn_w_up.astype(BF16), ffn_w_down.astype(BF16)
    moe_g, moe_u, moe_d = moe_w_gate.astype(BF16), moe_w_up.astype(BF16), moe_w_down.astype(BF16)

    mod_base = _ada_base(c, ada_w, ada_b)
    cos2, sin2 = _rope_tables(positions, dk)

    for l in range(depth):
        table = ada_table[l]
        hn = _norm_mod(x, mod_base, table, 0)
        proj = _matmul(hn.reshape(t, d), w_in_b, l).reshape(b, s, -1)
        y_ret = _retention(proj, cos2, sin2, ret_norm_g[l], dk, dv)
        y_lru = _rg_lru(proj, gate_off, x_off, conv_w[l], conv_b[l], wcat[l], gate_a_b[l], gate_x_b[l],
                        lru_lambda[l], lru_norm_g[l])
        x = _matmul_residual([y_ret.reshape(t, ret_v), y_lru.reshape(t, lru_w)], [w_out_ret, w_out_lru], l,
                             x.reshape(t, d), mod_base, table, 2, s, TM_MM, "out_proj").reshape(b, s, d)
        if l % 2 == 0:
            hn = _norm_mod(x, mod_base, table, 3)
            hmid = _swiglu_up(hn.reshape(t, d), ffn_g, ffn_u, l // 2)
            x = _matmul_residual([hmid], [ffn_d], l // 2, x.reshape(t, d), mod_base, table, 5, s, TM_DOWN,
                                 "ffn_down").reshape(b, s, d)
        else:
            hn, comb, mask = _router(x, mod_base, table, 3, router_w[l // 2])
            comb = comb.reshape(t, LANES)[:, :n_experts]
            mask = mask.reshape(t, LANES)[:, :n_experts]
            src_tok, row_w, pos, tile_expert, n_used = _route_plan(mask, comb, tm_moe)
            xs = _gather_rows(hn.reshape(t, d), src_tok)
            hmid = _moe_up(xs, moe_g, moe_u, l // 2, tile_expert, n_used, tm_moe)
            y = _moe_down(hmid, moe_d, l // 2, row_w, tile_expert, n_used, tm_moe)
            x = _moe_combine(y, pos, x.reshape(t, d), mod_base, table, 5, s).reshape(b, s, d)
    return _final_norm(x, final_norm_g)
```

```python
import functools

import jax
import jax.numpy as jnp
from jax import lax
from jax.experimental import pallas as pl
from jax.experimental.pallas import tpu as pltpu

F32 = jnp.float32
BF16 = jnp.bfloat16
U32 = jnp.uint32

RET_HEADS = 8
RET_CHUNK = 128
ROPE_BASE = 10000.0
LRU_C = 8.0
N_MOD = 6
EPS = 1e-6

VMEM_LIMIT_BYTES = 56 * 1024 * 1024
LANES = 128

TM_MM = 1024
TN_MM = 1024
TN_RES = 512
TM_DOWN = 512
TN_FFN = 512
TS_NORM = 256
R_RET = 1024
R_LRU = 256
TM_MOE = 512
TN_MOE_UP = 256
G_DISPATCH = 512
G_ROWS = 128
TN_ADA = 512


def _tile(dim, pref):
    if dim <= pref:
        return dim
    for t in range(pref, 0, -1):
        if dim % t == 0 and t % LANES == 0:
            return t
    for t in range(pref, 0, -1):
        if dim % t == 0 and t % 8 == 0:
            return t
    return dim


def _params(*sem):
    return pltpu.CompilerParams(dimension_semantics=sem, vmem_limit_bytes=VMEM_LIMIT_BYTES)


def _silu(x):
    return x * jax.nn.sigmoid(x)


def _dot(a, b):
    return jnp.dot(a, b, preferred_element_type=F32)


def _ada_kernel(c_ref, w_ref, b_ref, o_ref):
    c = c_ref[...]
    o_ref[...] = _dot(_silu(c).astype(BF16), w_ref[...].astype(BF16)) + b_ref[...]


def _ada_base(c, ada_w, ada_b):
    b, d = c.shape
    n = ada_w.shape[1]
    rows = 8
    cp = jnp.zeros((rows, d), F32).at[:b].set(c)
    tn = _tile(n, TN_ADA)
    out = pl.pallas_call(
        _ada_kernel,
        grid=(n // tn,),
        in_specs=[pl.BlockSpec((rows, d), lambda j: (0, 0)),
                  pl.BlockSpec((d, tn), lambda j: (0, j)),
                  pl.BlockSpec((1, tn), lambda j: (0, j))],
        out_specs=pl.BlockSpec((rows, tn), lambda j: (0, j)),
        out_shape=jax.ShapeDtypeStruct((rows, n), F32),
        compiler_params=_params("arbitrary"),
        name="ada_base",
    )(cp, ada_w, ada_b.reshape(1, n))
    return out[:b].reshape(b, N_MOD, d)


def _rope_kernel(pos_ref, inv_ref, sign_ref, cos_ref, sin_ref):
    ang = pos_ref[...].astype(F32) * inv_ref[...]
    cos_ref[...] = jnp.cos(ang)
    sin_ref[...] = jnp.sin(ang) * sign_ref[...]


def _rope_tables(positions, dk):
    b, s = positions.shape
    inv = jnp.exp2(-jnp.arange(0, dk, 2, dtype=F32) / dk * jnp.log2(jnp.float32(ROPE_BASE)))
    inv2 = jnp.concatenate([inv, inv]).reshape(1, dk)
    sign = jnp.concatenate([-jnp.ones((dk // 2,), F32), jnp.ones((dk // 2,), F32)]).reshape(1, dk)
    ts = _tile(s, 1024)
    spec = pl.BlockSpec((None, ts, dk), lambda i, j: (i, j, 0))
    return pl.pallas_call(
        _rope_kernel,
        grid=(b, s // ts),
        in_specs=[pl.BlockSpec((None, ts, 1), lambda i, j: (i, j, 0)),
                  pl.BlockSpec((1, dk), lambda i, j: (0, 0)),
                  pl.BlockSpec((1, dk), lambda i, j: (0, 0))],
        out_specs=[spec, spec],
        out_shape=[jax.ShapeDtypeStruct((b, s, dk), F32)] * 2,
        compiler_params=_params("arbitrary", "arbitrary"),
        name="rope_tables",
    )(positions.reshape(b, s, 1), inv2, sign)


def _mod_rows(mb_ref, tb_ref, j):
    return mb_ref[j:j + 1, :] + tb_ref[j:j + 1, :]


def _normed(x):
    return x * lax.rsqrt(jnp.mean(x * x, axis=-1, keepdims=True) + EPS)


def _norm_mod_kernel(x_ref, mb_ref, tb_ref, o_ref, *, j_shift):
    xn = _normed(x_ref[...])
    shift = _mod_rows(mb_ref, tb_ref, j_shift)
    scale = _mod_rows(mb_ref, tb_ref, j_shift + 1)
    o_ref[...] = (xn * (1.0 + scale) + shift).astype(o_ref.dtype)


def _norm_mod(x, mod_base, table, j_shift):
    b, s, d = x.shape
    ts = _tile(s, TS_NORM)
    return pl.pallas_call(
        functools.partial(_norm_mod_kernel, j_shift=j_shift),
        grid=(b, s // ts),
        in_specs=[pl.BlockSpec((None, ts, d), lambda i, j: (i, j, 0)),
                  pl.BlockSpec((None, N_MOD, d), lambda i, j: (i, 0, 0)),
                  pl.BlockSpec((N_MOD, d), lambda i, j: (0, 0))],
        out_specs=pl.BlockSpec((None, ts, d), lambda i, j: (i, j, 0)),
        out_shape=jax.ShapeDtypeStruct((b, s, d), BF16),
        compiler_params=_params("arbitrary", "arbitrary"),
        name="norm_mod",
    )(x, mod_base, table)


def _final_norm_kernel(x_ref, g_ref, o_ref):
    o_ref[...] = _normed(x_ref[...]) * g_ref[...]


def _final_norm(x, gain):
    b, s, d = x.shape
    ts = _tile(s, TS_NORM)
    return pl.pallas_call(
        _final_norm_kernel,
        grid=(b, s // ts),
        in_specs=[pl.BlockSpec((None, ts, d), lambda i, j: (i, j, 0)),
                  pl.BlockSpec((1, d), lambda i, j: (0, 0))],
        out_specs=pl.BlockSpec((None, ts, d), lambda i, j: (i, j, 0)),
        out_shape=jax.ShapeDtypeStruct((b, s, d), F32),
        compiler_params=_params("arbitrary", "arbitrary"),
        name="final_norm",
    )(x, gain.reshape(1, d))


def _pack_bf16_pairs(x):
    half = x.shape[1] // 2
    lo = pltpu.bitcast(x[:, :half].astype(BF16).astype(F32), U32) >> 16
    hi = pltpu.bitcast(x[:, half:].astype(BF16).astype(F32), U32) & jnp.uint32(0xFFFF0000)
    return hi | lo


def _unpack_slab(w):
    return pltpu.bitcast(w << 16, F32), pltpu.bitcast(w & jnp.uint32(0xFFFF0000), F32)


def _store_slabs(ref, w):
    rows = w.shape[0]
    n = w.shape[1] // LANES
    for c in range(n):
        ref[pl.ds(c, rows, stride=n), :] = w[:, c * LANES:(c + 1) * LANES]


def _router_kernel(x_ref, mb_ref, tb_ref, rw_ref, tri_ref, hn_ref, comb_ref, rank_ref, count_ref,
                   *, j_shift, n_experts):
    @pl.when(jnp.logical_and(pl.program_id(0) == 0, pl.program_id(1) == 0))
    def _():
        count_ref[...] = jnp.zeros_like(count_ref)

    xn = _normed(x_ref[...])
    shift = _mod_rows(mb_ref, tb_ref, j_shift)
    scale = _mod_rows(mb_ref, tb_ref, j_shift + 1)
    hn = xn * (1.0 + scale) + shift
    _store_slabs(hn_ref, _pack_bf16_pairs(hn))
    h_hi = hn.astype(BF16)
    h_lo = (hn - h_hi.astype(F32)).astype(BF16)
    rw = rw_ref[...]
    w_hi = rw.astype(BF16)
    w_lo = (rw - w_hi.astype(F32)).astype(BF16)
    logits = _dot(h_hi, w_hi) + (_dot(h_lo, w_hi) + _dot(h_hi, w_lo))
    lane = lax.broadcasted_iota(jnp.int32, logits.shape, 1)
    neg = jnp.float32(-jnp.inf)
    l1 = jnp.where(lane < n_experts, logits, neg)
    m1 = jnp.max(l1, axis=-1, keepdims=True)
    i1 = jnp.min(jnp.where(l1 == m1, lane, LANES), axis=-1, keepdims=True)
    sel1 = lane == i1
    l2 = jnp.where(sel1, neg, l1)
    m2 = jnp.max(l2, axis=-1, keepdims=True)
    i2 = jnp.min(jnp.where(l2 == m2, lane, LANES), axis=-1, keepdims=True)
    sel2 = lane == i2
    e2 = jnp.exp(m2 - m1)
    den = 1.0 + e2
    comb_ref[...] = jnp.where(sel1, 1.0 / den, 0.0) + jnp.where(sel2, e2 / den, 0.0)
    sel = jnp.logical_or(sel1, sel2)
    self32 = jnp.where(sel, 1.0, 0.0)
    prefix = _dot(tri_ref[...], self32.astype(BF16))
    rank_ref[...] = jnp.where(sel, count_ref[0:1, :] + prefix - 1.0, -1.0)
    count_ref[0:1, :] = count_ref[0:1, :] + jnp.sum(self32, axis=0, keepdims=True)


def _router(x, mod_base, table, j_shift, router_w):
    b, s, d = x.shape
    e = router_w.shape[1]
    rw = jnp.zeros((d, LANES), F32).at[:, :e].set(router_w)
    ts = _tile(s, TS_NORM)
    tri = (jnp.arange(ts)[:, None] >= jnp.arange(ts)[None, :]).astype(BF16)
    n_slab = d // (2 * LANES)
    lane_out = pl.BlockSpec((None, ts, LANES), lambda i, j: (i, j, 0))
    return pl.pallas_call(
        functools.partial(_router_kernel, j_shift=j_shift, n_experts=e),
        grid=(b, s // ts),
        in_specs=[pl.BlockSpec((None, ts, d), lambda i, j: (i, j, 0)),
                  pl.BlockSpec((None, N_MOD, d), lambda i, j: (i, 0, 0)),
                  pl.BlockSpec((N_MOD, d), lambda i, j: (0, 0)),
                  pl.BlockSpec((d, LANES), lambda i, j: (0, 0)),
                  pl.BlockSpec((ts, ts), lambda i, j: (0, 0))],
        out_specs=[pl.BlockSpec((None, ts * n_slab, LANES), lambda i, j: (i, j, 0)), lane_out, lane_out],
        out_shape=[jax.ShapeDtypeStruct((b, s * n_slab, LANES), U32),
                   jax.ShapeDtypeStruct((b, s, LANES), F32),
                   jax.ShapeDtypeStruct((b, s, LANES), F32)],
        scratch_shapes=[pltpu.VMEM((8, LANES), F32)],
        compiler_params=_params("arbitrary", "arbitrary"),
        name="router",
    )(x, mod_base, table, rw, tri)


def _mm_kernel(a_ref, w_ref, o_ref):
    o_ref[...] = _dot(a_ref[...], w_ref[...]).astype(o_ref.dtype)


def _matmul(a, w_stack, layer):
    m, k = a.shape
    n = w_stack.shape[2]
    tm, tn = _tile(m, TM_MM), _tile(n, TN_MM)
    return pl.pallas_call(
        _mm_kernel,
        grid=(m // tm, n // tn),
        in_specs=[pl.BlockSpec((tm, k), lambda i, j: (i, 0)),
                  pl.BlockSpec((None, k, tn), lambda i, j: (layer, 0, j))],
        out_specs=pl.BlockSpec((tm, tn), lambda i, j: (i, j)),
        out_shape=jax.ShapeDtypeStruct((m, n), BF16),
        compiler_params=_params("arbitrary", "arbitrary"),
        name="in_proj",
    )(a, w_stack)


def _mm_res_kernel(*refs, n_lhs, j_gate):
    a_refs = refs[:n_lhs]
    w_refs = refs[n_lhs:2 * n_lhs]
    x_ref, mb_ref, tb_ref, o_ref = refs[2 * n_lhs:]
    acc = _dot(a_refs[0][...], w_refs[0][...])
    for a_ref, w_ref in zip(a_refs[1:], w_refs[1:]):
        acc = acc + _dot(a_ref[...], w_ref[...])
    o_ref[...] = x_ref[...] + _mod_rows(mb_ref, tb_ref, j_gate) * acc


def _matmul_residual(lhs, w_stacks, layer, x, mod_base, table, j_gate, seq, tm_pref, name):
    m, n = x.shape
    tm, tn = _tile(seq, tm_pref), _tile(n, TN_RES)
    in_specs = [pl.BlockSpec((tm, a.shape[1]), lambda i, j: (i, 0)) for a in lhs]
    in_specs += [pl.BlockSpec((None, w.shape[1], tn), lambda i, j: (layer, 0, j)) for w in w_stacks]
    in_specs += [pl.BlockSpec((tm, tn), lambda i, j: (i, j)),
                 pl.BlockSpec((None, N_MOD, tn), lambda i, j: ((i * tm) // seq, 0, j)),
                 pl.BlockSpec((N_MOD, tn), lambda i, j: (0, j))]
    return pl.pallas_call(
        functools.partial(_mm_res_kernel, n_lhs=len(lhs), j_gate=j_gate),
        grid=(m // tm, n // tn),
        in_specs=in_specs,
        out_specs=pl.BlockSpec((tm, tn), lambda i, j: (i, j)),
        out_shape=jax.ShapeDtypeStruct((m, n), F32),
        compiler_params=_params("arbitrary", "arbitrary"),
        name=name,
    )(*lhs, *w_stacks, x, mod_base, table)


def _swiglu_kernel(a_ref, wg_ref, wu_ref, o_ref):
    a = a_ref[...]
    g = _dot(a, wg_ref[...])
    u = _dot(a, wu_ref[...])
    o_ref[...] = (_silu(g) * u).astype(o_ref.dtype)


def _swiglu_up(a, wg_stack, wu_stack, idx):
    m, k = a.shape
    n = wg_stack.shape[2]
    tm, tn = _tile(m, TM_MM), _tile(n, TN_FFN)
    wspec = pl.BlockSpec((None, k, tn), lambda i, j: (idx, 0, j))
    return pl.pallas_call(
        _swiglu_kernel,
        grid=(m // tm, n // tn),
        in_specs=[pl.BlockSpec((tm, k), lambda i, j: (i, 0)), wspec, wspec],
        out_specs=pl.BlockSpec((tm, tn), lambda i, j: (i, j)),
        out_shape=jax.ShapeDtypeStruct((m, n), BF16),
        compiler_params=_params("arbitrary", "arbitrary"),
        name="ffn_up",
    )(a, wg_stack, wu_stack)


def _retention_kernel(lg_ref, q_ref, k_ref, v_ref, g_ref, cos_ref, sin_ref, gain_ref, o_ref, state_ref,
                      *, n_chunks, chunk, dk):
    @pl.when(pl.program_id(2) == 0)
    def _():
        state_ref[...] = jnp.zeros_like(state_ref)

    lg = lg_ref[0:1, 0:1]
    row = lax.broadcasted_iota(jnp.int32, (chunk, chunk), 0)
    col = lax.broadcasted_iota(jnp.int32, (chunk, chunk), 1)
    rel = (row - col).astype(F32)
    decay = jnp.where(rel >= 0, jnp.exp(lg * jnp.maximum(rel, 0.0)), 0.0)
    ri = lax.broadcasted_iota(jnp.int32, (chunk, 1), 0).astype(F32)
    q_decay = jnp.exp(lg * (ri + 1.0))
    k_decay = jnp.exp(lg * (chunk - 1.0 - ri))
    chunk_decay = jnp.exp(lg * float(chunk))
    gain = gain_ref[...]
    k_scale = dk ** -0.5

    for c in range(n_chunks):
        rows = slice(c * chunk, (c + 1) * chunk)
        cos, sin = cos_ref[rows, :], sin_ref[rows, :]
        q = q_ref[rows, :].astype(F32)
        k = k_ref[rows, :].astype(F32)
        q = q * cos + pltpu.roll(q, dk // 2, axis=1) * sin
        k = (k * cos + pltpu.roll(k, dk // 2, axis=1) * sin) * k_scale
        qb = q.astype(BF16)
        vb = v_ref[rows, :]
        scores = lax.dot_general(qb, k.astype(BF16), (((1,), (1,)), ((), ())),
                                 preferred_element_type=F32) * decay
        state = state_ref[...]
        y = _dot(scores.astype(BF16), vb) + _dot(qb, state.astype(BF16)) * q_decay
        kd = (k * k_decay).astype(BF16)
        state_ref[...] = state * chunk_decay + lax.dot_general(
            kd, vb, (((0,), (0,)), ((), ())), preferred_element_type=F32)
        yc = y - jnp.mean(y, axis=-1, keepdims=True)
        yn = yc * lax.rsqrt(jnp.mean(yc * yc, axis=-1, keepdims=True) + EPS)
        g = g_ref[rows, :].astype(F32)
        o_ref[rows, :] = (_silu(g) * (yn * gain)).astype(o_ref.dtype)


def _retention(proj, cos2, sin2, gain, dk, dv):
    b, s, _ = proj.shape
    h = RET_HEADS
    r = _tile(s, R_RET)
    assert r % RET_CHUNK == 0 and (h * dk) % dv == 0
    log_gamma = jnp.log1p(-jnp.exp2(-5.0 - jnp.arange(h, dtype=F32)))
    lg = jnp.broadcast_to(log_gamma[:, None, None], (h, 8, LANES))
    k_blk, v_blk, g_blk = h, 2 * h * dk // dv, 2 * h * dk // dv + h
    return pl.pallas_call(
        functools.partial(_retention_kernel, n_chunks=r // RET_CHUNK, chunk=RET_CHUNK, dk=dk),
        grid=(b, h, s // r),
        in_specs=[pl.BlockSpec((None, 8, LANES), lambda i, j, t: (j, 0, 0)),
                  pl.BlockSpec((None, r, dk), lambda i, j, t: (i, t, j)),
                  pl.BlockSpec((None, r, dk), lambda i, j, t: (i, t, k_blk + j)),
                  pl.BlockSpec((None, r, dv), lambda i, j, t: (i, t, v_blk + j)),
                  pl.BlockSpec((None, r, dv), lambda i, j, t: (i, t, g_blk + j)),
                  pl.BlockSpec((None, r, dk), lambda i, j, t: (i, t, 0)),
                  pl.BlockSpec((None, r, dk), lambda i, j, t: (i, t, 0)),
                  pl.BlockSpec((1, dv), lambda i, j, t: (0, j))],
        out_specs=pl.BlockSpec((None, r, dv), lambda i, j, t: (i, t, j)),
        out_shape=jax.ShapeDtypeStruct((b, s, h * dv), BF16),
        scratch_shapes=[pltpu.VMEM((dk, dv), F32)],
        compiler_params=_params("arbitrary", "arbitrary", "arbitrary"),
        name="retention",
    )(lg, proj, proj, proj, proj, cos2, sin2, gain.reshape(1, h * dv))


def _gelu_tanh(x):
    return 0.5 * x * (1.0 + jnp.tanh(0.7978845608028654 * (x + 0.044715 * (x * x * x))))


def _softplus(z):
    return jnp.maximum(z, 0.0) + jnp.log1p(jnp.exp(-jnp.abs(z)))


def _scan_rows(a, u):
    n = a.shape[0]
    row = lax.broadcasted_iota(jnp.int32, a.shape, 0)
    d = 1
    while d < n:
        keep = row >= d
        a_prev = pltpu.roll(a, d, axis=0)
        u_prev = pltpu.roll(u, d, axis=0)
        u = jnp.where(keep, a * u_prev, 0.0) + u
        a = jnp.where(keep, a * a_prev, a)
        d *= 2
    return a, u


def _lru_kernel(x_ref, gate_ref, cw_ref, cb_ref, wcat_ref, ba_ref, bx_ref, lam_ref, gn_ref, o_ref,
                xs_ref, hs_ref, hn_ref, hprev_ref, xtail_ref, *, n_blocks, bw, conv_width, seg, pitch):
    rows = x_ref.shape[0]
    width = x_ref.shape[1]
    halo = 8
    assert rows == 8 * seg and pitch == seg + halo and conv_width - 1 <= halo

    @pl.when(pl.program_id(1) == 0)
    def _():
        xtail_ref[...] = jnp.zeros_like(xtail_ref)
        hprev_ref[...] = jnp.zeros_like(hprev_ref)

    sub = lax.broadcasted_iota(jnp.int32, (8, bw), 0)
    ss = [jnp.zeros((8, 1), F32) for _ in range(seg)]
    for n in range(n_blocks):
        cols = slice(n * bw, (n + 1) * bw)
        slab = xs_ref.at[n]
        xf = x_ref[:, cols].astype(F32)
        slab[0:halo, :] = xtail_ref[:, cols]
        for s in range(8):
            slab[s * pitch + halo:s * pitch + halo + seg, :] = xf[s * seg:(s + 1) * seg, :]
            if s > 0:
                slab[s * pitch:s * pitch + halo, :] = xf[s * seg - halo:s * seg, :]
        xtail_ref[:, cols] = xf[rows - halo:rows, :]
        first = halo - (conv_width - 1)
        taps = [slab[pl.ds(first + o, 8, stride=pitch), :] for o in range(seg + conv_width - 1)]
        cw = [cw_ref[j:j + 1, cols] for j in range(conv_width)]
        cb = cb_ref[:, cols]
        xc_rows = []
        for g in range(seg):
            acc = cb + taps[g] * cw[0]
            for j in range(1, conv_width):
                acc = acc + taps[g + j] * cw[j]
            xc_rows.append(acc)
        xc = jnp.concatenate(xc_rows, axis=0)
        gates = _dot(xc.astype(BF16), wcat_ref[n])
        r = jax.nn.sigmoid(gates[:, :bw] + ba_ref[:, cols])
        i = jax.nn.sigmoid(gates[:, bw:] + bx_ref[:, cols])
        log_a = (-LRU_C) * r * _softplus(-lam_ref[:, cols])
        a = jnp.exp(log_a)
        u = jnp.sqrt(jnp.maximum(1.0 - a * a, 0.0)) * (i * xc)
        h_loc = [u[0:8, :]]
        a_cum = [a[0:8, :]]
        for g in range(1, seg):
            a_g = a[g * 8:(g + 1) * 8, :]
            h_loc.append(a_g * h_loc[-1] + u[g * 8:(g + 1) * 8, :])
            a_cum.append(a_g * a_cum[-1])
        h_in = jnp.broadcast_to(hprev_ref[0:1, cols], (8, bw))
        f_prev = jnp.where(sub == 0, h_in, pltpu.roll(h_loc[-1], 1, axis=0))
        p_prev = jnp.where(sub == 0, 0.0, pltpu.roll(a_cum[-1], 1, axis=0))
        _, carry = _scan_rows(p_prev, f_prev)
        hprev_ref[0:1, cols] = (h_loc[-1] + a_cum[-1] * carry)[7:8, :]
        slab_h = hs_ref.at[n]
        for g in range(seg):
            h_g = h_loc[g] + a_cum[g] * carry
            slab_h[g * 8:(g + 1) * 8, :] = h_g
            ss[g] = ss[g] + jnp.sum(h_g * h_g, axis=-1, keepdims=True)
    inv = [lax.rsqrt(ss_g / width + EPS) for ss_g in ss]
    for n in range(n_blocks):
        cols = slice(n * bw, (n + 1) * bw)
        slab_h = hs_ref.at[n]
        gn = gn_ref[:, cols]
        for g in range(seg):
            hn_ref[pl.ds(g, 8, stride=pitch), :] = slab_h[g * 8:(g + 1) * 8, :] * inv[g] * gn
        for s in range(8):
            t_rows = slice(s * seg, (s + 1) * seg)
            gate = gate_ref[t_rows, cols].astype(F32)
            o_ref[t_rows, cols] = (_gelu_tanh(gate) * hn_ref[s * pitch:s * pitch + seg, :]).astype(o_ref.dtype)


def _rg_lru(proj, gate_off, x_off, conv_w, conv_b, wcat, b_a, b_x, lam, gnorm):
    b, s, _ = proj.shape
    n_blocks, bw, _ = wcat.shape
    w = n_blocks * bw
    assert gate_off % w == 0 and x_off % w == 0
    r = _tile(s, R_LRU)
    seg = r // 8
    pitch = seg + 8
    assert r % 128 == 0 and bw == LANES
    conv_width = conv_w.shape[0]
    vec = pl.BlockSpec((1, w), lambda i, t: (0, 0))
    return pl.pallas_call(
        functools.partial(_lru_kernel, n_blocks=n_blocks, bw=bw, conv_width=conv_width, seg=seg, pitch=pitch),
        grid=(b, s // r),
        in_specs=[pl.BlockSpec((None, r, w), lambda i, t: (i, t, x_off // w)),
                  pl.BlockSpec((None, r, w), lambda i, t: (i, t, gate_off // w)),
                  pl.BlockSpec((conv_width, w), lambda i, t: (0, 0)),
                  vec,
                  pl.BlockSpec((n_blocks, bw, 2 * bw), lambda i, t: (0, 0, 0)),
                  vec, vec, vec, vec],
        out_specs=pl.BlockSpec((None, r, w), lambda i, t: (i, t, 0)),
        out_shape=jax.ShapeDtypeStruct((b, s, w), BF16),
        scratch_shapes=[pltpu.VMEM((n_blocks, 8 * pitch, bw), F32), pltpu.VMEM((n_blocks, r, bw), F32),
                        pltpu.VMEM((8 * pitch, bw), F32), pltpu.VMEM((8, w), F32), pltpu.VMEM((8, w), F32)],
        compiler_params=_params("arbitrary", "arbitrary"),
        name="rg_lru",
    )(proj, proj, conv_w, conv_b.reshape(1, w), wcat, b_a.reshape(1, w), b_x.reshape(1, w),
      lam.reshape(1, w), gnorm.reshape(1, w))


def _dispatch_kernel(pos_ref, pad_lo_ref, pad_hi_ref, nu_ref, src_ref, xs_ref, zero_ref, sem,
                     *, g, n_tok, n_experts, tm, n_tiles):
    step = pl.program_id(0)
    base = step * g

    def row_copy(r, k):
        dst = pos_ref[k * n_tok + base + r]
        return pltpu.make_async_copy(src_ref.at[pl.ds(r, 1)], xs_ref.at[pl.ds(dst, 1)], sem.at[0])

    def start(r, carry):
        row_copy(r, 0).start()
        row_copy(r, 1).start()
        return carry

    def wait(r, carry):
        row_copy(r, 0).wait()
        row_copy(r, 1).wait()
        return carry

    lax.fori_loop(0, g, start, 0)

    @pl.when(step == 0)
    def _():
        zero_ref[...] = jnp.zeros_like(zero_ref)

        def pad_copy(p):
            return pltpu.make_async_copy(zero_ref.at[pl.ds(0, 1)], xs_ref.at[pl.ds(p, 1)], sem.at[1])

        def tile_copy(t):
            return pltpu.make_async_copy(zero_ref, xs_ref.at[pl.ds(t * tm, tm)], sem.at[1])

        def run(lo, hi, make):
            def go(p, carry):
                make(p).start()
                return carry

            def done(p, carry):
                make(p).wait()
                return carry

            lax.fori_loop(lo, hi, go, 0)
            lax.fori_loop(lo, hi, done, 0)

        for e in range(n_experts):
            run(pad_lo_ref[e], pad_hi_ref[e], pad_copy)
        run(nu_ref[0], n_tiles, tile_copy)

    lax.fori_loop(0, g, wait, 0)


def _dispatch(hn_packed, pos, pad_lo, pad_hi, n_used, tm, n_rows):
    t, n_slab, _ = hn_packed.shape
    g = _tile(t, G_DISPATCH)
    n_experts = pad_lo.shape[0]
    return pl.pallas_call(
        functools.partial(_dispatch_kernel, g=g, n_tok=t, n_experts=n_experts, tm=tm, n_tiles=n_rows // tm),
        grid_spec=pltpu.PrefetchScalarGridSpec(
            num_scalar_prefetch=4,
            grid=(t // g,),
            in_specs=[pl.BlockSpec((g, n_slab, LANES), lambda i, *_: (i, 0, 0))],
            out_specs=pl.BlockSpec(memory_space=pl.ANY),
            scratch_shapes=[pltpu.VMEM((tm, n_slab, LANES), U32), pltpu.SemaphoreType.DMA((2,))]),
        out_shape=jax.ShapeDtypeStruct((n_rows, n_slab, LANES), U32),
        compiler_params=_params("arbitrary"),
        name="moe_dispatch",
    )(pos, pad_lo, pad_hi, n_used, hn_packed)


def _moe_ffn_kernel(te_ref, nu_ref, xs_ref, wg_ref, wu_ref, wd_ref, y_ref, abf_ref, acc_ref):
    j = pl.program_id(1)
    used = pl.program_id(0) < nu_ref[0]
    tm = abf_ref.shape[0]
    n_slab = y_ref.shape[0] // tm
    half = n_slab * LANES

    @pl.when(jnp.logical_and(used, j == 0))
    def _():
        for c in range(n_slab):
            lo, hi = _unpack_slab(xs_ref[pl.ds(c, tm, stride=n_slab), :])
            abf_ref[:, c * LANES:(c + 1) * LANES] = lo.astype(BF16)
            abf_ref[:, half + c * LANES:half + (c + 1) * LANES] = hi.astype(BF16)

    @pl.when(used)
    def _():
        a = abf_ref[...]
        g = _dot(a, wg_ref[...])
        u = _dot(a, wu_ref[...])
        contrib = _dot((_silu(g) * u).astype(BF16), wd_ref[...])

        @pl.when(j == 0)
        def _():
            acc_ref[...] = contrib

        @pl.when(j > 0)
        def _():
            acc_ref[...] += contrib

    @pl.when(jnp.logical_and(used, j == pl.num_programs(1) - 1))
    def _():
        _store_slabs(y_ref, _pack_bf16_pairs(acc_ref[...]))

    @pl.when(jnp.logical_and(jnp.logical_not(used), j == 0))
    def _():
        y_ref[...] = jnp.zeros_like(y_ref)


def _moe_ffn(xs, wg_stack, wu_stack, wd_stack, layer_idx, tile_expert, n_used, tm):
    p, n_slab, _ = xs.shape
    d = 2 * n_slab * LANES
    de = wg_stack.shape[3]
    tn = _tile(de, TN_MOE_UP)
    row_in = pl.BlockSpec((tm * n_slab, LANES), lambda i, j, te, nu: (jnp.minimum(i, nu[0] - 1), 0))
    w_up = pl.BlockSpec((None, None, d, tn), lambda i, j, te, nu: (layer_idx, te[i], 0, j))
    w_down = pl.BlockSpec((None, None, tn, d), lambda i, j, te, nu: (layer_idx, te[i], j, 0))
    return pl.pallas_call(
        _moe_ffn_kernel,
        grid_spec=pltpu.PrefetchScalarGridSpec(
            num_scalar_prefetch=2,
            grid=(p // tm, de // tn),
            in_specs=[row_in, w_up, w_up, w_down],
            out_specs=pl.BlockSpec((tm * n_slab, LANES), lambda i, j, te, nu: (i, 0)),
            scratch_shapes=[pltpu.VMEM((tm, d), BF16), pltpu.VMEM((tm, d), F32)]),
        out_shape=jax.ShapeDtypeStruct((p * n_slab, LANES), U32),
        compiler_params=_params("arbitrary", "arbitrary"),
        name="moe_ffn",
    )(tile_expert, n_used, xs.reshape(p * n_slab, LANES), wg_stack, wu_stack, wd_stack)


def _combine_kernel(pos_ref, y_ref, x_ref, w_ref, mb_ref, tb_ref, o_ref, buf_ref, sem, *, g, n_tok, j_gate):
    step = pl.program_id(0)
    n_slab = buf_ref.shape[2] // g

    def for_rows(block, slot, act):
        base = block * g

        def body(r, carry):
            for k in range(2):
                src = pl.multiple_of(pos_ref[k * n_tok + base + r] * n_slab, n_slab)
                dst = pl.multiple_of(r * n_slab, n_slab)
                act(pltpu.make_async_copy(y_ref.at[pl.ds(src, n_slab)],
                                          buf_ref.at[slot, k, pl.ds(dst, n_slab)], sem.at[slot]))
            return carry

        lax.fori_loop(0, g, body, 0)

    @pl.when(step == 0)
    def _():
        for_rows(0, 0, lambda cp: cp.start())

    @pl.when(step + 1 < pl.num_programs(0))
    def _():
        for_rows(step + 1, (step + 1) % 2, lambda cp: cp.start())

    slot = step % 2
    for_rows(step, slot, lambda cp: cp.wait())

    gate = _mod_rows(mb_ref, tb_ref, j_gate)
    w_a, w_b = w_ref[:, 0:1], w_ref[:, 1:2]
    half = n_slab * LANES
    rows_a, rows_b = buf_ref.at[slot, 0], buf_ref.at[slot, 1]
    for c in range(n_slab):
        a_lo, a_hi = _unpack_slab(rows_a[pl.ds(c, g, stride=n_slab), :])
        b_lo, b_hi = _unpack_slab(rows_b[pl.ds(c, g, stride=n_slab), :])
        lo = slice(c * LANES, (c + 1) * LANES)
        hi = slice(half + c * LANES, half + (c + 1) * LANES)
        o_ref[:, lo] = x_ref[:, lo] + gate[:, lo] * (w_a * a_lo + w_b * b_lo)
        o_ref[:, hi] = x_ref[:, hi] + gate[:, hi] * (w_a * a_hi + w_b * b_hi)


def _moe_combine(y, pos, w2, x, mod_base, table, j_gate, seq):
    t, d = x.shape
    n_slab = d // (2 * LANES)
    g = _tile(seq, G_ROWS)
    return pl.pallas_call(
        functools.partial(_combine_kernel, g=g, n_tok=t, j_gate=j_gate),
        grid_spec=pltpu.PrefetchScalarGridSpec(
            num_scalar_prefetch=1,
            grid=(t // g,),
            in_specs=[pl.BlockSpec(memory_space=pl.ANY),
                      pl.BlockSpec((g, d), lambda i, pos_ref: (i, 0)),
                      pl.BlockSpec((g, 2), lambda i, pos_ref: (i, 0)),
                      pl.BlockSpec((None, N_MOD, d), lambda i, pos_ref: ((i * g) // seq, 0, 0)),
                      pl.BlockSpec((N_MOD, d), lambda i, pos_ref: (0, 0))],
            out_specs=pl.BlockSpec((g, d), lambda i, pos_ref: (i, 0)),
            scratch_shapes=[pltpu.VMEM((2, 2, g * n_slab, LANES), U32), pltpu.SemaphoreType.DMA((2,))]),
        out_shape=jax.ShapeDtypeStruct((t, d), F32),
        compiler_params=_params("arbitrary"),
        name="moe_combine",
    )(pos, y, x, w2, mod_base, table)


def _route_plan(rank, comb, tm):
    t, e = rank.shape
    sel = rank >= 0.0
    counts = jnp.sum(sel.astype(jnp.int32), axis=0)
    padded = ((counts + tm - 1) // tm) * tm
    ends = jnp.cumsum(padded)
    starts = ends - padded
    dest = starts[None, :] + rank.astype(jnp.int32)
    pos_a = jnp.min(jnp.where(sel, dest, jnp.int32(2 ** 30)), axis=1)
    pos_b = jnp.max(jnp.where(sel, dest, jnp.int32(-1)), axis=1)
    w_a = jnp.sum(jnp.where(jnp.logical_and(sel, dest == pos_a[:, None]), comb, 0.0), axis=1)
    w_b = jnp.sum(jnp.where(jnp.logical_and(sel, dest == pos_b[:, None]), comb, 0.0), axis=1)
    pos = jnp.concatenate([pos_a, pos_b]).astype(jnp.int32)
    w2 = jnp.stack([w_a, w_b], axis=1)
    n_rows = 2 * t + e * tm
    n_tiles = n_rows // tm
    tile_start = jnp.arange(n_tiles, dtype=jnp.int32) * tm
    n_used = (ends[-1] // tm).astype(jnp.int32)
    te_raw = jnp.sum((tile_start[:, None] >= ends[None, :]).astype(jnp.int32), axis=1)
    last_e = jnp.sum((ends[-1] - 1 >= ends).astype(jnp.int32))
    tile_expert = jnp.where(tile_start < ends[-1], jnp.minimum(te_raw, e - 1), last_e).astype(jnp.int32)
    return (pos, w2, tile_expert, n_used.reshape(1), (starts + counts).astype(jnp.int32),
            ends.astype(jnp.int32), n_rows)


def kernel(x, c, positions, ada_w, ada_b, ada_table, w_in, conv_w, conv_b, gate_a_w, gate_a_b, gate_x_w,
           gate_x_b, lru_lambda, ret_norm_g, lru_norm_g, w_out, ffn_w_gate, ffn_w_up, ffn_w_down, router_w,
           moe_w_gate, moe_w_up, moe_w_down, final_norm_g):
    b, s, d = x.shape
    depth = w_in.shape[0]
    t = b * s
    ret_v = ret_norm_g.shape[1]
    lru_w = lru_norm_g.shape[1]
    ret_qk = (w_in.shape[2] - 2 * ret_v - 2 * lru_w) // 2
    dk, dv = ret_qk // RET_HEADS, ret_v // RET_HEADS
    gate_off = 2 * ret_qk + 2 * ret_v
    x_off = gate_off + lru_w
    n_experts = router_w.shape[2]
    tm_moe = _tile(t, TM_MOE)

    w_in_b = w_in.astype(BF16)
    w_out_ret = w_out[:, :ret_v].astype(BF16)
    w_out_lru = w_out[:, ret_v:].astype(BF16)
    wcat = jnp.concatenate([gate_a_w, gate_x_w], axis=-1).astype(BF16)
    ffn_g, ffn_u, ffn_d = ffn_w_gate.astype(BF16), ffn_w_up.astype(BF16), ffn_w_down.astype(BF16)
    moe_g, moe_u, moe_d = moe_w_gate.astype(BF16), moe_w_up.astype(BF16), moe_w_down.astype(BF16)

    mod_base = _ada_base(c, ada_w, ada_b)
    cos2, sin2 = _rope_tables(positions, dk)

    for l in range(depth):
        table = ada_table[l]
        hn = _norm_mod(x, mod_base, table, 0)
        proj = _matmul(hn.reshape(t, d), w_in_b, l).reshape(b, s, -1)
        y_ret = _retention(proj, cos2, sin2, ret_norm_g[l], dk, dv)
        y_lru = _rg_lru(proj, gate_off, x_off, conv_w[l], conv_b[l], wcat[l], gate_a_b[l], gate_x_b[l],
                        lru_lambda[l], lru_norm_g[l])
        x = _matmul_residual([y_ret.reshape(t, ret_v), y_lru.reshape(t, lru_w)], [w_out_ret, w_out_lru], l,
                             x.reshape(t, d), mod_base, table, 2, s, TM_MM, "out_proj").reshape(b, s, d)
        if l % 2 == 0:
            hn = _norm_mod(x, mod_base, table, 3)
            hmid = _swiglu_up(hn.reshape(t, d), ffn_g, ffn_u, l // 2)
            x = _matmul_residual([hmid], [ffn_d], l // 2, x.reshape(t, d), mod_base, table, 5, s, TM_DOWN,
                                 "ffn_down").reshape(b, s, d)
        else:
            hn_packed, comb, rank = _router(x, mod_base, table, 3, router_w[l // 2])
            comb = comb.reshape(t, LANES)[:, :n_experts]
            rank = rank.reshape(t, LANES)[:, :n_experts]
            pos, w2, tile_expert, n_used, pad_lo, pad_hi, n_rows = _route_plan(rank, comb, tm_moe)
            xs = _dispatch(hn_packed.reshape(t, -1, LANES), pos, pad_lo, pad_hi, n_used, tm_moe, n_rows)
            y = _moe_ffn(xs, moe_g, moe_u, moe_d, l // 2, tile_expert, n_used, tm_moe)
            x = _moe_combine(y, pos, w2, x.reshape(t, d), mod_base, table, 5, s).reshape(b, s, d)
    return _final_norm(x, final_norm_g)
```

```python
import functools

import jax
import jax.numpy as jnp
from jax import lax
from jax.experimental import pallas as pl
from jax.experimental.pallas import tpu as pltpu

F32 = jnp.float32
BF16 = jnp.bfloat16
U32 = jnp.uint32

RET_HEADS = 8
RET_CHUNK = 128
ROPE_BASE = 10000.0
LRU_C = 8.0
N_MOD = 6
EPS = 1e-6

VMEM_LIMIT_BYTES = 56 * 1024 * 1024
LANES = 128

TM_MM = 1024
TN_MM = 1024
TN_RES = 1024
TK_RES = 2048
TN_FFN = 512
TS_NORM = 512
TS_ROUTER = 256
R_RET = 1024
R_LRU = 256
TM_MOE = 512
TN_MOE_UP = 256
TN_MOE_DOWN = 1024
DISPATCH_LAG = 512
G_ROWS = 128
TN_ADA = 512


def _tile(dim, pref):
    if dim <= pref:
        return dim
    for t in range(pref, 0, -1):
        if dim % t == 0 and t % LANES == 0:
            return t
    for t in range(pref, 0, -1):
        if dim % t == 0 and t % 8 == 0:
            return t
    return dim


def _params(*sem):
    return pltpu.CompilerParams(dimension_semantics=sem, vmem_limit_bytes=VMEM_LIMIT_BYTES)


def _silu(x):
    return x * jax.nn.sigmoid(x)


def _dot(a, b):
    return jnp.dot(a, b, preferred_element_type=F32)


def _ada_kernel(c_ref, w_ref, b_ref, o_ref):
    c = c_ref[...]
    o_ref[...] = _dot(_silu(c).astype(BF16), w_ref[...].astype(BF16)) + b_ref[...]


def _ada_base(c, ada_w, ada_b):
    b, d = c.shape
    n = ada_w.shape[1]
    rows = 8
    cp = jnp.zeros((rows, d), F32).at[:b].set(c)
    tn = _tile(n, TN_ADA)
    out = pl.pallas_call(
        _ada_kernel,
        grid=(n // tn,),
        in_specs=[pl.BlockSpec((rows, d), lambda j: (0, 0)),
                  pl.BlockSpec((d, tn), lambda j: (0, j)),
                  pl.BlockSpec((1, tn), lambda j: (0, j))],
        out_specs=pl.BlockSpec((rows, tn), lambda j: (0, j)),
        out_shape=jax.ShapeDtypeStruct((rows, n), F32),
        compiler_params=_params("arbitrary"),
        name="ada_base",
    )(cp, ada_w, ada_b.reshape(1, n))
    return out[:b].reshape(b, N_MOD, d)


def _rope_kernel(pos_ref, inv_ref, sign_ref, cos_ref, sin_ref):
    ang = pos_ref[...].astype(F32) * inv_ref[...]
    cos_ref[...] = jnp.cos(ang)
    sin_ref[...] = jnp.sin(ang) * sign_ref[...]


def _rope_tables(positions, dk):
    b, s = positions.shape
    inv = jnp.exp2(-jnp.arange(0, dk, 2, dtype=F32) / dk * jnp.log2(jnp.float32(ROPE_BASE)))
    inv2 = jnp.concatenate([inv, inv]).reshape(1, dk)
    sign = jnp.concatenate([-jnp.ones((dk // 2,), F32), jnp.ones((dk // 2,), F32)]).reshape(1, dk)
    ts = _tile(s, 1024)
    spec = pl.BlockSpec((None, ts, dk), lambda i, j: (i, j, 0))
    return pl.pallas_call(
        _rope_kernel,
        grid=(b, s // ts),
        in_specs=[pl.BlockSpec((None, ts, 1), lambda i, j: (i, j, 0)),
                  pl.BlockSpec((1, dk), lambda i, j: (0, 0)),
                  pl.BlockSpec((1, dk), lambda i, j: (0, 0))],
        out_specs=[spec, spec],
        out_shape=[jax.ShapeDtypeStruct((b, s, dk), F32)] * 2,
        compiler_params=_params("arbitrary", "arbitrary"),
        name="rope_tables",
    )(positions.reshape(b, s, 1), inv2, sign)


def _mod_rows(mb_ref, tb_ref, j):
    return mb_ref[j:j + 1, :] + tb_ref[j:j + 1, :]


def _normed(x):
    return x * lax.rsqrt(jnp.mean(x * x, axis=-1, keepdims=True) + EPS)


def _norm_mod_kernel(x_ref, mb_ref, tb_ref, o_ref, *, j_shift):
    xn = _normed(x_ref[...])
    shift = _mod_rows(mb_ref, tb_ref, j_shift)
    scale = _mod_rows(mb_ref, tb_ref, j_shift + 1)
    o_ref[...] = (xn * (1.0 + scale) + shift).astype(o_ref.dtype)


def _norm_mod(x, mod_base, table, j_shift):
    b, s, d = x.shape
    ts = _tile(s, TS_NORM)
    return pl.pallas_call(
        functools.partial(_norm_mod_kernel, j_shift=j_shift),
        grid=(b, s // ts),
        in_specs=[pl.BlockSpec((None, ts, d), lambda i, j: (i, j, 0)),
                  pl.BlockSpec((None, N_MOD, d), lambda i, j: (i, 0, 0)),
                  pl.BlockSpec((N_MOD, d), lambda i, j: (0, 0))],
        out_specs=pl.BlockSpec((None, ts, d), lambda i, j: (i, j, 0)),
        out_shape=jax.ShapeDtypeStruct((b, s, d), BF16),
        compiler_params=_params("arbitrary", "arbitrary"),
        name="norm_mod",
    )(x, mod_base, table)


def _final_norm_kernel(x_ref, g_ref, o_ref):
    o_ref[...] = _normed(x_ref[...]) * g_ref[...]


def _final_norm(x, gain):
    b, s, d = x.shape
    ts = _tile(s, TS_NORM)
    return pl.pallas_call(
        _final_norm_kernel,
        grid=(b, s // ts),
        in_specs=[pl.BlockSpec((None, ts, d), lambda i, j: (i, j, 0)),
                  pl.BlockSpec((1, d), lambda i, j: (0, 0))],
        out_specs=pl.BlockSpec((None, ts, d), lambda i, j: (i, j, 0)),
        out_shape=jax.ShapeDtypeStruct((b, s, d), F32),
        compiler_params=_params("arbitrary", "arbitrary"),
        name="final_norm",
    )(x, gain.reshape(1, d))


def _pack_bf16_pairs(x):
    half = x.shape[1] // 2
    lo = pltpu.bitcast(x[:, :half].astype(BF16).astype(F32), U32) >> 16
    hi = pltpu.bitcast(x[:, half:].astype(BF16).astype(F32), U32) & jnp.uint32(0xFFFF0000)
    return hi | lo


def _unpack_slab(w):
    return pltpu.bitcast(w << 16, F32), pltpu.bitcast(w & jnp.uint32(0xFFFF0000), F32)


def _store_slabs(ref, w):
    rows = w.shape[0]
    n = w.shape[1] // LANES
    for c in range(n):
        ref[pl.ds(c, rows, stride=n), :] = w[:, c * LANES:(c + 1) * LANES]


def _router_kernel(x_ref, mb_ref, tb_ref, rw_ref, tri_ref, hn_ref, comb_ref, rank_ref, count_ref,
                   *, j_shift, n_experts):
    @pl.when(jnp.logical_and(pl.program_id(0) == 0, pl.program_id(1) == 0))
    def _():
        count_ref[...] = jnp.zeros_like(count_ref)

    xn = _normed(x_ref[...])
    shift = _mod_rows(mb_ref, tb_ref, j_shift)
    scale = _mod_rows(mb_ref, tb_ref, j_shift + 1)
    hn = xn * (1.0 + scale) + shift
    _store_slabs(hn_ref, _pack_bf16_pairs(hn))
    h_hi = hn.astype(BF16)
    h_lo = (hn - h_hi.astype(F32)).astype(BF16)
    rw = rw_ref[...]
    w_hi = rw.astype(BF16)
    w_lo = (rw - w_hi.astype(F32)).astype(BF16)
    logits = _dot(h_hi, w_hi) + (_dot(h_lo, w_hi) + _dot(h_hi, w_lo))
    lane = lax.broadcasted_iota(jnp.int32, logits.shape, 1)
    neg = jnp.float32(-jnp.inf)
    l1 = jnp.where(lane < n_experts, logits, neg)
    m1 = jnp.max(l1, axis=-1, keepdims=True)
    i1 = jnp.min(jnp.where(l1 == m1, lane, LANES), axis=-1, keepdims=True)
    sel1 = lane == i1
    l2 = jnp.where(sel1, neg, l1)
    m2 = jnp.max(l2, axis=-1, keepdims=True)
    i2 = jnp.min(jnp.where(l2 == m2, lane, LANES), axis=-1, keepdims=True)
    sel2 = lane == i2
    e2 = jnp.exp(m2 - m1)
    den = 1.0 + e2
    comb_ref[...] = jnp.where(sel1, 1.0 / den, 0.0) + jnp.where(sel2, e2 / den, 0.0)
    sel = jnp.logical_or(sel1, sel2)
    self32 = jnp.where(sel, 1.0, 0.0)
    prefix = _dot(tri_ref[...], self32.astype(BF16))
    rank_ref[...] = jnp.where(sel, count_ref[0:1, :] + prefix - 1.0, -1.0)
    count_ref[0:1, :] = count_ref[0:1, :] + jnp.sum(self32, axis=0, keepdims=True)


def _router(x, mod_base, table, j_shift, router_w):
    b, s, d = x.shape
    e = router_w.shape[1]
    rw = jnp.zeros((d, LANES), F32).at[:, :e].set(router_w)
    ts = _tile(s, TS_ROUTER)
    tri = (jnp.arange(ts)[:, None] >= jnp.arange(ts)[None, :]).astype(BF16)
    n_slab = d // (2 * LANES)
    lane_out = pl.BlockSpec((None, ts, LANES), lambda i, j: (i, j, 0))
    return pl.pallas_call(
        functools.partial(_router_kernel, j_shift=j_shift, n_experts=e),
        grid=(b, s // ts),
        in_specs=[pl.BlockSpec((None, ts, d), lambda i, j: (i, j, 0)),
                  pl.BlockSpec((None, N_MOD, d), lambda i, j: (i, 0, 0)),
                  pl.BlockSpec((N_MOD, d), lambda i, j: (0, 0)),
                  pl.BlockSpec((d, LANES), lambda i, j: (0, 0)),
                  pl.BlockSpec((ts, ts), lambda i, j: (0, 0))],
        out_specs=[pl.BlockSpec((None, ts * n_slab, LANES), lambda i, j: (i, j, 0)), lane_out, lane_out],
        out_shape=[jax.ShapeDtypeStruct((b, s * n_slab, LANES), U32),
                   jax.ShapeDtypeStruct((b, s, LANES), F32),
                   jax.ShapeDtypeStruct((b, s, LANES), F32)],
        scratch_shapes=[pltpu.VMEM((8, LANES), F32)],
        compiler_params=_params("arbitrary", "arbitrary"),
        name="router",
    )(x, mod_base, table, rw, tri)


def _mm_kernel(a_ref, w_ref, o_ref):
    o_ref[...] = _dot(a_ref[...], w_ref[...]).astype(o_ref.dtype)


def _matmul(a, w_stack, layer):
    m, k = a.shape
    n = w_stack.shape[2]
    tm, tn = _tile(m, TM_MM), _tile(n, TN_MM)
    return pl.pallas_call(
        _mm_kernel,
        grid=(m // tm, n // tn),
        in_specs=[pl.BlockSpec((tm, k), lambda i, j: (i, 0)),
                  pl.BlockSpec((None, k, tn), lambda i, j: (layer, 0, j))],
        out_specs=pl.BlockSpec((tm, tn), lambda i, j: (i, j)),
        out_shape=jax.ShapeDtypeStruct((m, n), BF16),
        compiler_params=_params("arbitrary", "arbitrary"),
        name="in_proj",
    )(a, w_stack)


def _mm_res_kernel(*refs, n_lhs, j_gate):
    a_refs = refs[:n_lhs]
    w_refs = refs[n_lhs:2 * n_lhs]
    x_ref, mb_ref, tb_ref, o_ref = refs[2 * n_lhs:]
    k = pl.program_id(2)

    @pl.when(k == 0)
    def _():
        o_ref[...] = jnp.zeros_like(o_ref)

    for a_ref, w_ref in zip(a_refs, w_refs):
        o_ref[...] += _dot(a_ref[...], w_ref[...])

    @pl.when(k == pl.num_programs(2) - 1)
    def _():
        o_ref[...] = x_ref[...] + _mod_rows(mb_ref, tb_ref, j_gate) * o_ref[...]


def _matmul_residual(lhs, w_stacks, layer, x, mod_base, table, j_gate, seq, name):
    m, n = x.shape
    tm, tn = _tile(seq, TM_MM), _tile(n, TN_RES)
    nk = max(1, sum(a.shape[1] for a in lhs) // TK_RES)
    assert all(a.shape[1] % nk == 0 for a in lhs)
    in_specs = [pl.BlockSpec((tm, a.shape[1] // nk), lambda i, j, k: (i, k)) for a in lhs]
    in_specs += [pl.BlockSpec((None, w.shape[1] // nk, tn), lambda i, j, k: (layer, k, j)) for w in w_stacks]
    in_specs += [pl.BlockSpec((tm, tn), lambda i, j, k: (i, j)),
                 pl.BlockSpec((None, N_MOD, tn), lambda i, j, k: ((i * tm) // seq, 0, j)),
                 pl.BlockSpec((N_MOD, tn), lambda i, j, k: (0, j))]
    return pl.pallas_call(
        functools.partial(_mm_res_kernel, n_lhs=len(lhs), j_gate=j_gate),
        grid=(m // tm, n // tn, nk),
        in_specs=in_specs,
        out_specs=pl.BlockSpec((tm, tn), lambda i, j, k: (i, j)),
        out_shape=jax.ShapeDtypeStruct((m, n), F32),
        compiler_params=_params("arbitrary", "arbitrary", "arbitrary"),
        name=name,
    )(*lhs, *w_stacks, x, mod_base, table)


def _swiglu_kernel(a_ref, wg_ref, wu_ref, o_ref):
    a = a_ref[...]
    g = _dot(a, wg_ref[...])
    u = _dot(a, wu_ref[...])
    o_ref[...] = (_silu(g) * u).astype(o_ref.dtype)


def _swiglu_up(a, wg_stack, wu_stack, idx):
    m, k = a.shape
    n = wg_stack.shape[2]
    tm, tn = _tile(m, TM_MM), _tile(n, TN_FFN)
    wspec = pl.BlockSpec((None, k, tn), lambda i, j: (idx, 0, j))
    return pl.pallas_call(
        _swiglu_kernel,
        grid=(m // tm, n // tn),
        in_specs=[pl.BlockSpec((tm, k), lambda i, j: (i, 0)), wspec, wspec],
        out_specs=pl.BlockSpec((tm, tn), lambda i, j: (i, j)),
        out_shape=jax.ShapeDtypeStruct((m, n), BF16),
        compiler_params=_params("arbitrary", "arbitrary"),
        name="ffn_up",
    )(a, wg_stack, wu_stack)


def _retention_kernel(lg_ref, q_ref, k_ref, v_ref, g_ref, cos_ref, sin_ref, gain_ref, o_ref, state_ref,
                      *, n_chunks, chunk, dk):
    @pl.when(pl.program_id(2) == 0)
    def _():
        state_ref[...] = jnp.zeros_like(state_ref)

    lg = lg_ref[0:1, 0:1]
    row = lax.broadcasted_iota(jnp.int32, (chunk, chunk), 0)
    col = lax.broadcasted_iota(jnp.int32, (chunk, chunk), 1)
    rel = (row - col).astype(F32)
    decay = jnp.where(rel >= 0, jnp.exp(lg * jnp.maximum(rel, 0.0)), 0.0)
    ri = lax.broadcasted_iota(jnp.int32, (chunk, 1), 0).astype(F32)
    q_decay = jnp.exp(lg * (ri + 1.0))
    k_decay = jnp.exp(lg * (chunk - 1.0 - ri))
    chunk_decay = jnp.exp(lg * float(chunk))
    gain = gain_ref[...]
    k_scale = dk ** -0.5

    for c in range(n_chunks):
        rows = slice(c * chunk, (c + 1) * chunk)
        cos, sin = cos_ref[rows, :], sin_ref[rows, :]
        q = q_ref[rows, :].astype(F32)
        k = k_ref[rows, :].astype(F32)
        q = q * cos + pltpu.roll(q, dk // 2, axis=1) * sin
        k = (k * cos + pltpu.roll(k, dk // 2, axis=1) * sin) * k_scale
        qb = q.astype(BF16)
        vb = v_ref[rows, :]
        scores = lax.dot_general(qb, k.astype(BF16), (((1,), (1,)), ((), ())),
                                 preferred_element_type=F32) * decay
        state = state_ref[...]
        y = _dot(scores.astype(BF16), vb) + _dot(qb, state.astype(BF16)) * q_decay
        kd = (k * k_decay).astype(BF16)
        state_ref[...] = state * chunk_decay + lax.dot_general(
            kd, vb, (((0,), (0,)), ((), ())), preferred_element_type=F32)
        yc = y - jnp.mean(y, axis=-1, keepdims=True)
        yn = yc * lax.rsqrt(jnp.mean(yc * yc, axis=-1, keepdims=True) + EPS)
        g = g_ref[rows, :].astype(F32)
        o_ref[rows, :] = (_silu(g) * (yn * gain)).astype(o_ref.dtype)


def _retention(proj, cos2, sin2, gain, dk, dv):
    b, s, _ = proj.shape
    h = RET_HEADS
    r = _tile(s, R_RET)
    assert r % RET_CHUNK == 0 and (h * dk) % dv == 0
    log_gamma = jnp.log1p(-jnp.exp2(-5.0 - jnp.arange(h, dtype=F32)))
    lg = jnp.broadcast_to(log_gamma[:, None, None], (h, 8, LANES))
    k_blk, v_blk, g_blk = h, 2 * h * dk // dv, 2 * h * dk // dv + h
    return pl.pallas_call(
        functools.partial(_retention_kernel, n_chunks=r // RET_CHUNK, chunk=RET_CHUNK, dk=dk),
        grid=(b, h, s // r),
        in_specs=[pl.BlockSpec((None, 8, LANES), lambda i, j, t: (j, 0, 0)),
                  pl.BlockSpec((None, r, dk), lambda i, j, t: (i, t, j)),
                  pl.BlockSpec((None, r, dk), lambda i, j, t: (i, t, k_blk + j)),
                  pl.BlockSpec((None, r, dv), lambda i, j, t: (i, t, v_blk + j)),
                  pl.BlockSpec((None, r, dv), lambda i, j, t: (i, t, g_blk + j)),
                  pl.BlockSpec((None, r, dk), lambda i, j, t: (i, t, 0)),
                  pl.BlockSpec((None, r, dk), lambda i, j, t: (i, t, 0)),
                  pl.BlockSpec((1, dv), lambda i, j, t: (0, j))],
        out_specs=pl.BlockSpec((None, r, dv), lambda i, j, t: (i, t, j)),
        out_shape=jax.ShapeDtypeStruct((b, s, h * dv), BF16),
        scratch_shapes=[pltpu.VMEM((dk, dv), F32)],
        compiler_params=_params("arbitrary", "arbitrary", "arbitrary"),
        name="retention",
    )(lg, proj, proj, proj, proj, cos2, sin2, gain.reshape(1, h * dv))


def _gelu_tanh(x):
    return 0.5 * x * (1.0 + jnp.tanh(0.7978845608028654 * (x + 0.044715 * (x * x * x))))


def _softplus(z):
    return jnp.maximum(z, 0.0) + jnp.log1p(jnp.exp(-jnp.abs(z)))


def _scan_rows(a, u):
    n = a.shape[0]
    row = lax.broadcasted_iota(jnp.int32, a.shape, 0)
    d = 1
    while d < n:
        keep = row >= d
        a_prev = pltpu.roll(a, d, axis=0)
        u_prev = pltpu.roll(u, d, axis=0)
        u = jnp.where(keep, a * u_prev, 0.0) + u
        a = jnp.where(keep, a * a_prev, a)
        d *= 2
    return a, u


def _lru_kernel(x_ref, gate_ref, cw_ref, cb_ref, wcat_ref, ba_ref, bx_ref, lam_ref, gn_ref, o_ref,
                xs_ref, hs_ref, hn_ref, hprev_ref, xtail_ref, *, n_blocks, bw, conv_width, seg, pitch):
    rows = x_ref.shape[0]
    width = x_ref.shape[1]
    halo = 8
    assert rows == 8 * seg and pitch == seg + halo and conv_width - 1 <= halo

    @pl.when(pl.program_id(1) == 0)
    def _():
        xtail_ref[...] = jnp.zeros_like(xtail_ref)
        hprev_ref[...] = jnp.zeros_like(hprev_ref)

    sub = lax.broadcasted_iota(jnp.int32, (8, bw), 0)
    ss = [jnp.zeros((8, 1), F32) for _ in range(seg)]
    for n in range(n_blocks):
        cols = slice(n * bw, (n + 1) * bw)
        slab = xs_ref.at[n]
        xf = x_ref[:, cols].astype(F32)
        slab[0:halo, :] = xtail_ref[:, cols]
        for s in range(8):
            slab[s * pitch + halo:s * pitch + halo + seg, :] = xf[s * seg:(s + 1) * seg, :]
            if s > 0:
                slab[s * pitch:s * pitch + halo, :] = xf[s * seg - halo:s * seg, :]
        xtail_ref[:, cols] = xf[rows - halo:rows, :]
        first = halo - (conv_width - 1)
        taps = [slab[pl.ds(first + o, 8, stride=pitch), :] for o in range(seg + conv_width - 1)]
        cw = [cw_ref[j:j + 1, cols] for j in range(conv_width)]
        cb = cb_ref[:, cols]
        xc_rows = []
        for g in range(seg):
            acc = cb + taps[g] * cw[0]
            for j in range(1, conv_width):
                acc = acc + taps[g + j] * cw[j]
            xc_rows.append(acc)
        xc = jnp.concatenate(xc_rows, axis=0)
        gates = _dot(xc.astype(BF16), wcat_ref[n])
        r = jax.nn.sigmoid(gates[:, :bw] + ba_ref[:, cols])
        i = jax.nn.sigmoid(gates[:, bw:] + bx_ref[:, cols])
        log_a = (-LRU_C) * r * _softplus(-lam_ref[:, cols])
        a = jnp.exp(log_a)
        u = jnp.sqrt(jnp.maximum(1.0 - a * a, 0.0)) * (i * xc)
        h_loc = [u[0:8, :]]
        a_cum = [a[0:8, :]]
        for g in range(1, seg):
            a_g = a[g * 8:(g + 1) * 8, :]
            h_loc.append(a_g * h_loc[-1] + u[g * 8:(g + 1) * 8, :])
            a_cum.append(a_g * a_cum[-1])
        h_in = jnp.broadcast_to(hprev_ref[0:1, cols], (8, bw))
        f_prev = jnp.where(sub == 0, h_in, pltpu.roll(h_loc[-1], 1, axis=0))
        p_prev = jnp.where(sub == 0, 0.0, pltpu.roll(a_cum[-1], 1, axis=0))
        _, carry = _scan_rows(p_prev, f_prev)
        hprev_ref[0:1, cols] = (h_loc[-1] + a_cum[-1] * carry)[7:8, :]
        slab_h = hs_ref.at[n]
        for g in range(seg):
            h_g = h_loc[g] + a_cum[g] * carry
            slab_h[g * 8:(g + 1) * 8, :] = h_g
            ss[g] = ss[g] + jnp.sum(h_g * h_g, axis=-1, keepdims=True)
    inv = [lax.rsqrt(ss_g / width + EPS) for ss_g in ss]
    for n in range(n_blocks):
        cols = slice(n * bw, (n + 1) * bw)
        slab_h = hs_ref.at[n]
        gn = gn_ref[:, cols]
        for g in range(seg):
            hn_ref[pl.ds(g, 8, stride=pitch), :] = slab_h[g * 8:(g + 1) * 8, :] * inv[g] * gn
        for s in range(8):
            t_rows = slice(s * seg, (s + 1) * seg)
            gate = gate_ref[t_rows, cols].astype(F32)
            o_ref[t_rows, cols] = (_gelu_tanh(gate) * hn_ref[s * pitch:s * pitch + seg, :]).astype(o_ref.dtype)


def _rg_lru(proj, gate_off, x_off, conv_w, conv_b, wcat, b_a, b_x, lam, gnorm):
    b, s, _ = proj.shape
    n_blocks, bw, _ = wcat.shape
    w = n_blocks * bw
    assert gate_off % w == 0 and x_off % w == 0
    r = _tile(s, R_LRU)
    seg = r // 8
    pitch = seg + 8
    assert r % 128 == 0 and bw == LANES
    conv_width = conv_w.shape[0]
    vec = pl.BlockSpec((1, w), lambda i, t: (0, 0))
    return pl.pallas_call(
        functools.partial(_lru_kernel, n_blocks=n_blocks, bw=bw, conv_width=conv_width, seg=seg, pitch=pitch),
        grid=(b, s // r),
        in_specs=[pl.BlockSpec((None, r, w), lambda i, t: (i, t, x_off // w)),
                  pl.BlockSpec((None, r, w), lambda i, t: (i, t, gate_off // w)),
                  pl.BlockSpec((conv_width, w), lambda i, t: (0, 0)),
                  vec,
                  pl.BlockSpec((n_blocks, bw, 2 * bw), lambda i, t: (0, 0, 0)),
                  vec, vec, vec, vec],
        out_specs=pl.BlockSpec((None, r, w), lambda i, t: (i, t, 0)),
        out_shape=jax.ShapeDtypeStruct((b, s, w), BF16),
        scratch_shapes=[pltpu.VMEM((n_blocks, 8 * pitch, bw), F32), pltpu.VMEM((n_blocks, r, bw), F32),
                        pltpu.VMEM((8 * pitch, bw), F32), pltpu.VMEM((8, w), F32), pltpu.VMEM((8, w), F32)],
        compiler_params=_params("arbitrary", "arbitrary"),
        name="rg_lru",
    )(proj, proj, conv_w, conv_b.reshape(1, w), wcat, b_a.reshape(1, w), b_x.reshape(1, w),
      lam.reshape(1, w), gnorm.reshape(1, w))


def _dispatch_kernel(pos_ref, pad_lo_ref, pad_hi_ref, nu_ref, src_ref, xs_ref, zero_ref, sem,
                     *, lag, n_tok, n_experts, tm, n_tiles):
    step = pl.program_id(0)

    def row_copy(t, k):
        return pltpu.make_async_copy(src_ref.at[pl.ds(t - step * lag, 1)],
                                     xs_ref.at[pl.ds(pos_ref[k * n_tok + t], 1)], sem.at[0])

    def start(t, carry):
        row_copy(t, 0).start()
        row_copy(t, 1).start()
        return carry

    def wait(t, carry):
        row_copy(t, 0).wait()
        row_copy(t, 1).wait()
        return carry

    lax.fori_loop(step * lag, (step + 1) * lag, start, 0)

    @pl.when(step == 0)
    def _():
        zero_ref[...] = jnp.zeros_like(zero_ref)

        def pad_copy(p):
            return pltpu.make_async_copy(zero_ref.at[pl.ds(0, 1)], xs_ref.at[pl.ds(p, 1)], sem.at[1])

        def tile_copy(t):
            return pltpu.make_async_copy(zero_ref, xs_ref.at[pl.ds(t * tm, tm)], sem.at[1])

        def run(lo, hi, make):
            def go(p, carry):
                make(p).start()
                return carry

            def done(p, carry):
                make(p).wait()
                return carry

            lax.fori_loop(lo, hi, go, 0)
            lax.fori_loop(lo, hi, done, 0)

        for e in range(n_experts):
            run(pad_lo_ref[e], pad_hi_ref[e], pad_copy)
        run(nu_ref[0], n_tiles, tile_copy)

    lax.fori_loop(step * lag, (step + 1) * lag, wait, 0)


def _dispatch(hn_packed, pos, pad_lo, pad_hi, n_used, tm, n_rows):
    t, n_slab, _ = hn_packed.shape
    n_experts = pad_lo.shape[0]
    lag = _tile(t, DISPATCH_LAG)
    return pl.pallas_call(
        functools.partial(_dispatch_kernel, lag=lag, n_tok=t, n_experts=n_experts, tm=tm, n_tiles=n_rows // tm),
        grid_spec=pltpu.PrefetchScalarGridSpec(
            num_scalar_prefetch=4,
            grid=(t // lag,),
            in_specs=[pl.BlockSpec((lag, n_slab, LANES), lambda i, *_: (i, 0, 0))],
            out_specs=pl.BlockSpec(memory_space=pl.ANY),
            scratch_shapes=[pltpu.VMEM((tm, n_slab, LANES), U32), pltpu.SemaphoreType.DMA((2,))]),
        out_shape=jax.ShapeDtypeStruct((n_rows, n_slab, LANES), U32),
        compiler_params=_params("arbitrary"),
        name="moe_dispatch",
    )(pos, pad_lo, pad_hi, n_used, hn_packed)


def _moe_ffn_kernel(te_ref, nu_ref, xs_ref, wg_ref, wu_ref, wdlo_ref, wdhi_ref, y_ref, abf_ref, h_ref):
    j = pl.program_id(1)
    used = pl.program_id(0) < nu_ref[0]
    tm = abf_ref.shape[0]
    n_up = h_ref.shape[0]
    n_slab = y_ref.shape[0] // tm
    half = n_slab * LANES
    slabs_per_step = wdlo_ref.shape[1] // LANES

    @pl.when(jnp.logical_and(used, j == 0))
    def _():
        for c in range(n_slab):
            lo, hi = _unpack_slab(xs_ref[pl.ds(c, tm, stride=n_slab), :])
            abf_ref[:, c * LANES:(c + 1) * LANES] = lo.astype(BF16)
            abf_ref[:, half + c * LANES:half + (c + 1) * LANES] = hi.astype(BF16)

    @pl.when(jnp.logical_and(used, j < n_up))
    def _():
        a = abf_ref[...]
        g = _dot(a, wg_ref[...])
        u = _dot(a, wu_ref[...])
        h_ref[j] = (_silu(g) * u).astype(BF16)

    @pl.when(jnp.logical_and(used, j >= n_up))
    def _():
        h = jnp.concatenate([h_ref[c] for c in range(n_up)], axis=1)
        lo = pltpu.bitcast(_dot(h, wdlo_ref[...]).astype(BF16).astype(F32), U32) >> 16
        hi = pltpu.bitcast(_dot(h, wdhi_ref[...]).astype(BF16).astype(F32), U32) & jnp.uint32(0xFFFF0000)
        packed = hi | lo
        first = (j - n_up) * slabs_per_step
        for c in range(slabs_per_step):
            y_ref[pl.ds(first + c, tm, stride=n_slab), :] = packed[:, c * LANES:(c + 1) * LANES]

    @pl.when(jnp.logical_and(jnp.logical_not(used), j == 0))
    def _():
        y_ref[...] = jnp.zeros_like(y_ref)


def _moe_ffn(xs, wg_stack, wu_stack, wd_stack, layer_idx, tile_expert, n_used, tm):
    p, n_slab, _ = xs.shape
    d = 2 * n_slab * LANES
    de = wg_stack.shape[3]
    tn = _tile(de, TN_MOE_UP)
    n_up = de // tn
    cw = _tile(d // 2, TN_MOE_DOWN)
    n_down = (d // 2) // cw
    row_in = pl.BlockSpec((tm * n_slab, LANES), lambda i, j, te, nu: (jnp.minimum(i, nu[0] - 1), 0))
    w_up = pl.BlockSpec((None, None, d, tn),
                        lambda i, j, te, nu: (layer_idx, te[i], 0, jnp.minimum(j, n_up - 1)))
    w_down_lo = pl.BlockSpec((None, None, de, cw),
                             lambda i, j, te, nu: (layer_idx, te[i], 0, jnp.maximum(j - n_up, 0)))
    w_down_hi = pl.BlockSpec((None, None, de, cw),
                             lambda i, j, te, nu: (layer_idx, te[i], 0, n_down + jnp.maximum(j - n_up, 0)))
    return pl.pallas_call(
        _moe_ffn_kernel,
        grid_spec=pltpu.PrefetchScalarGridSpec(
            num_scalar_prefetch=2,
            grid=(p // tm, n_up + n_down),
            in_specs=[row_in, w_up, w_up, w_down_lo, w_down_hi],
            out_specs=pl.BlockSpec((tm * n_slab, LANES), lambda i, j, te, nu: (i, 0)),
            scratch_shapes=[pltpu.VMEM((tm, d), BF16), pltpu.VMEM((n_up, tm, tn), BF16)]),
        out_shape=jax.ShapeDtypeStruct((p * n_slab, LANES), U32),
        compiler_params=_params("arbitrary", "arbitrary"),
        name="moe_ffn",
    )(tile_expert, n_used, xs.reshape(p * n_slab, LANES), wg_stack, wu_stack, wd_stack, wd_stack)


def _combine_kernel(pos_ref, y_ref, x_ref, w_ref, mb_ref, tb_ref, o_ref, buf_ref, sem, *, g, n_tok, j_gate):
    step = pl.program_id(0)
    n_slab = buf_ref.shape[2] // g

    def row_copies(base, r, slot):
        dst = pl.multiple_of(r * n_slab, n_slab)
        for k in range(2):
            src = pl.multiple_of(pos_ref[k * n_tok + base + r] * n_slab, n_slab)
            yield pltpu.make_async_copy(y_ref.at[pl.ds(src, n_slab)], buf_ref.at[slot, k, pl.ds(dst, n_slab)],
                                        sem.at[slot])

    def start_row(r, carry):
        for cp in row_copies(0, r, 0):
            cp.start()
        return carry

    @pl.when(step == 0)
    def _():
        lax.fori_loop(0, g, start_row, 0)

    slot = step % 2

    def wait_row(r, carry):
        for cp in row_copies(step * g, r, slot):
            cp.wait()
        return carry

    lax.fori_loop(0, g, wait_row, 0)

    gate = _mod_rows(mb_ref, tb_ref, j_gate)
    w_a, w_b = w_ref[:, 0:1], w_ref[:, 1:2]
    half = n_slab * LANES
    rows_a, rows_b = buf_ref.at[slot, 0], buf_ref.at[slot, 1]

    def start_next_row(r, carry):
        for cp in row_copies((step + 1) * g, r, 1 - slot):
            cp.start()
        return carry

    @pl.when(step + 1 < pl.num_programs(0))
    def _():
        lax.fori_loop(0, g, start_next_row, 0)

    for c in range(n_slab):
        a_lo, a_hi = _unpack_slab(rows_a[pl.ds(c, g, stride=n_slab), :])
        b_lo, b_hi = _unpack_slab(rows_b[pl.ds(c, g, stride=n_slab), :])
        lo = slice(c * LANES, (c + 1) * LANES)
        hi = slice(half + c * LANES, half + (c + 1) * LANES)
        o_ref[:, lo] = x_ref[:, lo] + gate[:, lo] * (w_a * a_lo + w_b * b_lo)
        o_ref[:, hi] = x_ref[:, hi] + gate[:, hi] * (w_a * a_hi + w_b * b_hi)


def _moe_combine(y, pos, w2, x, mod_base, table, j_gate, seq):
    t, d = x.shape
    n_slab = d // (2 * LANES)
    g = _tile(seq, G_ROWS)
    return pl.pallas_call(
        functools.partial(_combine_kernel, g=g, n_tok=t, j_gate=j_gate),
        grid_spec=pltpu.PrefetchScalarGridSpec(
            num_scalar_prefetch=1,
            grid=(t // g,),
            in_specs=[pl.BlockSpec(memory_space=pl.ANY),
                      pl.BlockSpec((g, d), lambda i, pos_ref: (i, 0)),
                      pl.BlockSpec((g, 2), lambda i, pos_ref: (i, 0)),
                      pl.BlockSpec((None, N_MOD, d), lambda i, pos_ref: ((i * g) // seq, 0, 0)),
                      pl.BlockSpec((N_MOD, d), lambda i, pos_ref: (0, 0))],
            out_specs=pl.BlockSpec((g, d), lambda i, pos_ref: (i, 0)),
            scratch_shapes=[pltpu.VMEM((2, 2, g * n_slab, LANES), U32), pltpu.SemaphoreType.DMA((2,))]),
        out_shape=jax.ShapeDtypeStruct((t, d), F32),
        compiler_params=_params("arbitrary"),
        name="moe_combine",
    )(pos, y, x, w2, mod_base, table)


def _route_plan(rank, comb, tm):
    t, e = rank.shape
    sel = rank >= 0.0
    counts = jnp.sum(sel.astype(jnp.int32), axis=0)
    padded = ((counts + tm - 1) // tm) * tm
    ends = jnp.cumsum(padded)
    starts = ends - padded
    dest = starts[None, :] + rank.astype(jnp.int32)
    pos_a = jnp.min(jnp.where(sel, dest, jnp.int32(2 ** 30)), axis=1)
    pos_b = jnp.max(jnp.where(sel, dest, jnp.int32(-1)), axis=1)
    w_a = jnp.sum(jnp.where(jnp.logical_and(sel, dest == pos_a[:, None]), comb, 0.0), axis=1)
    w_b = jnp.sum(jnp.where(jnp.logical_and(sel, dest == pos_b[:, None]), comb, 0.0), axis=1)
    pos = jnp.concatenate([pos_a, pos_b]).astype(jnp.int32)
    w2 = jnp.stack([w_a, w_b], axis=1)
    n_rows = 2 * t + e * tm
    n_tiles = n_rows // tm
    tile_start = jnp.arange(n_tiles, dtype=jnp.int32) * tm
    n_used = (ends[-1] // tm).astype(jnp.int32)
    te_raw = jnp.sum((tile_start[:, None] >= ends[None, :]).astype(jnp.int32), axis=1)
    last_e = jnp.sum((ends[-1] - 1 >= ends).astype(jnp.int32))
    tile_expert = jnp.where(tile_start < ends[-1], jnp.minimum(te_raw, e - 1), last_e).astype(jnp.int32)
    return (pos, w2, tile_expert, n_used.reshape(1), (starts + counts).astype(jnp.int32),
            ends.astype(jnp.int32), n_rows)


def kernel(x, c, positions, ada_w, ada_b, ada_table, w_in, conv_w, conv_b, gate_a_w, gate_a_b, gate_x_w,
           gate_x_b, lru_lambda, ret_norm_g, lru_norm_g, w_out, ffn_w_gate, ffn_w_up, ffn_w_down, router_w,
           moe_w_gate, moe_w_up, moe_w_down, final_norm_g):
    b, s, d = x.shape
    depth = w_in.shape[0]
    t = b * s
    ret_v = ret_norm_g.shape[1]
    lru_w = lru_norm_g.shape[1]
    ret_qk = (w_in.shape[2] - 2 * ret_v - 2 * lru_w) // 2
    dk, dv = ret_qk // RET_HEADS, ret_v // RET_HEADS
    gate_off = 2 * ret_qk + 2 * ret_v
    x_off = gate_off + lru_w
    n_experts = router_w.shape[2]
    tm_moe = _tile(t, TM_MOE)

    w_in_b = w_in.astype(BF16)
    w_out_ret = w_out[:, :ret_v].astype(BF16)
    w_out_lru = w_out[:, ret_v:].astype(BF16)
    wcat = jnp.concatenate([gate_a_w, gate_x_w], axis=-1).astype(BF16)
    ffn_g, ffn_u, ffn_d = ffn_w_gate.astype(BF16), ffn_w_up.astype(BF16), ffn_w_down.astype(BF16)
    moe_g, moe_u, moe_d = moe_w_gate.astype(BF16), moe_w_up.astype(BF16), moe_w_down.astype(BF16)

    mod_base = _ada_base(c, ada_w, ada_b)
    cos2, sin2 = _rope_tables(positions, dk)

    for l in range(depth):
        table = ada_table[l]
        hn = _norm_mod(x, mod_base, table, 0)
        proj = _matmul(hn.reshape(t, d), w_in_b, l).reshape(b, s, -1)
        y_ret = _retention(proj, cos2, sin2, ret_norm_g[l], dk, dv)
        y_lru = _rg_lru(proj, gate_off, x_off, conv_w[l], conv_b[l], wcat[l], gate_a_b[l], gate_x_b[l],
                        lru_lambda[l], lru_norm_g[l])
        x = _matmul_residual([y_ret.reshape(t, ret_v), y_lru.reshape(t, lru_w)], [w_out_ret, w_out_lru], l,
                             x.reshape(t, d), mod_base, table, 2, s, "out_proj").reshape(b, s, d)
        if l % 2 == 0:
            hn = _norm_mod(x, mod_base, table, 3)
            hmid = _swiglu_up(hn.reshape(t, d), ffn_g, ffn_u, l // 2)
            x = _matmul_residual([hmid], [ffn_d], l // 2, x.reshape(t, d), mod_base, table, 5, s,
                                 "ffn_down").reshape(b, s, d)
        else:
            hn_packed, comb, rank = _router(x, mod_base, table, 3, router_w[l // 2])
            comb = comb.reshape(t, LANES)[:, :n_experts]
            rank = rank.reshape(t, LANES)[:, :n_experts]
            pos, w2, tile_expert, n_used, pad_lo, pad_hi, n_rows = _route_plan(rank, comb, tm_moe)
            xs = _dispatch(hn_packed.reshape(t, -1, LANES), pos, pad_lo, pad_hi, n_used, tm_moe, n_rows)
            y = _moe_ffn(xs, moe_g, moe_u, moe_d, l // 2, tile_expert, n_used, tm_moe)
            x = _moe_combine(y, pos, w2, x.reshape(t, d), mod_base, table, 5, s).reshape(b, s, d)
    return _final_norm(x, final_norm_g)
```

```python
import functools

import jax
import jax.numpy as jnp
from jax import lax
from jax.experimental import pallas as pl
from jax.experimental.pallas import tpu as pltpu

F32 = jnp.float32
BF16 = jnp.bfloat16
U32 = jnp.uint32

RET_HEADS = 8
RET_CHUNK = 128
ROPE_BASE = 10000.0
LRU_C = 8.0
N_MOD = 6
EPS = 1e-6

VMEM_LIMIT_BYTES = 56 * 1024 * 1024
LANES = 128

TM_MM = 1024
TN_MM = 1024
TN_RES = 1024
TK_RES = 2048
TN_OUT = 512
TK_OUT = 4096
TN_FFN = 512
TS_NORM = 512
TS_ROUTER = 256
R_RET = 2048
R_LRU = 256
TM_MOE = 512
TN_MOE_UP = 256
TN_MOE_DOWN = 1024
DISPATCH_LAG = 1024
G_ROWS = 256
TN_ADA = 512


def _tile(dim, pref):
    if dim <= pref:
        return dim
    for t in range(pref, 0, -1):
        if dim % t == 0 and t % LANES == 0:
            return t
    for t in range(pref, 0, -1):
        if dim % t == 0 and t % 8 == 0:
            return t
    return dim


def _params(*sem):
    return pltpu.CompilerParams(dimension_semantics=sem, vmem_limit_bytes=VMEM_LIMIT_BYTES)


def _silu(x):
    return x * jax.nn.sigmoid(x)


def _dot(a, b):
    return jnp.dot(a, b, preferred_element_type=F32)


def _ada_kernel(c_ref, w_ref, b_ref, o_ref):
    c = c_ref[...]
    o_ref[...] = _dot(_silu(c).astype(BF16), w_ref[...].astype(BF16)) + b_ref[...]


def _ada_base(c, ada_w, ada_b):
    b, d = c.shape
    n = ada_w.shape[1]
    rows = 8
    cp = jnp.zeros((rows, d), F32).at[:b].set(c)
    tn = _tile(n, TN_ADA)
    out = pl.pallas_call(
        _ada_kernel,
        grid=(n // tn,),
        in_specs=[pl.BlockSpec((rows, d), lambda j: (0, 0)),
                  pl.BlockSpec((d, tn), lambda j: (0, j)),
                  pl.BlockSpec((1, tn), lambda j: (0, j))],
        out_specs=pl.BlockSpec((rows, tn), lambda j: (0, j)),
        out_shape=jax.ShapeDtypeStruct((rows, n), F32),
        compiler_params=_params("arbitrary"),
        name="ada_base",
    )(cp, ada_w, ada_b.reshape(1, n))
    return out[:b].reshape(b, N_MOD, d)


def _rope_kernel(pos_ref, inv_ref, sign_ref, cos_ref, sin_ref):
    ang = pos_ref[...].astype(F32) * inv_ref[...]
    cos_ref[...] = jnp.cos(ang)
    sin_ref[...] = jnp.sin(ang) * sign_ref[...]


def _rope_tables(positions, dk):
    b, s = positions.shape
    inv = jnp.exp2(-jnp.arange(0, dk, 2, dtype=F32) / dk * jnp.log2(jnp.float32(ROPE_BASE)))
    inv2 = jnp.concatenate([inv, inv]).reshape(1, dk)
    sign = jnp.concatenate([-jnp.ones((dk // 2,), F32), jnp.ones((dk // 2,), F32)]).reshape(1, dk)
    ts = _tile(s, 1024)
    spec = pl.BlockSpec((None, ts, dk), lambda i, j: (i, j, 0))
    return pl.pallas_call(
        _rope_kernel,
        grid=(b, s // ts),
        in_specs=[pl.BlockSpec((None, ts, 1), lambda i, j: (i, j, 0)),
                  pl.BlockSpec((1, dk), lambda i, j: (0, 0)),
                  pl.BlockSpec((1, dk), lambda i, j: (0, 0))],
        out_specs=[spec, spec],
        out_shape=[jax.ShapeDtypeStruct((b, s, dk), F32)] * 2,
        compiler_params=_params("arbitrary", "arbitrary"),
        name="rope_tables",
    )(positions.reshape(b, s, 1), inv2, sign)


def _mod_rows(mb_ref, tb_ref, j):
    return mb_ref[j:j + 1, :] + tb_ref[j:j + 1, :]


def _normed(x):
    return x * lax.rsqrt(jnp.mean(x * x, axis=-1, keepdims=True) + EPS)


def _norm_mod_kernel(x_ref, mb_ref, tb_ref, o_ref, *, j_shift):
    xn = _normed(x_ref[...])
    shift = _mod_rows(mb_ref, tb_ref, j_shift)
    scale = _mod_rows(mb_ref, tb_ref, j_shift + 1)
    o_ref[...] = (xn * (1.0 + scale) + shift).astype(o_ref.dtype)


def _norm_mod(x, mod_base, table, j_shift):
    b, s, d = x.shape
    ts = _tile(s, TS_NORM)
    return pl.pallas_call(
        functools.partial(_norm_mod_kernel, j_shift=j_shift),
        grid=(b, s // ts),
        in_specs=[pl.BlockSpec((None, ts, d), lambda i, j: (i, j, 0)),
                  pl.BlockSpec((None, N_MOD, d), lambda i, j: (i, 0, 0)),
                  pl.BlockSpec((N_MOD, d), lambda i, j: (0, 0))],
        out_specs=pl.BlockSpec((None, ts, d), lambda i, j: (i, j, 0)),
        out_shape=jax.ShapeDtypeStruct((b, s, d), BF16),
        compiler_params=_params("arbitrary", "arbitrary"),
        name="norm_mod",
    )(x, mod_base, table)


def _final_norm_kernel(x_ref, g_ref, o_ref):
    o_ref[...] = _normed(x_ref[...]) * g_ref[...]


def _final_norm(x, gain):
    b, s, d = x.shape
    ts = _tile(s, TS_NORM)
    return pl.pallas_call(
        _final_norm_kernel,
        grid=(b, s // ts),
        in_specs=[pl.BlockSpec((None, ts, d), lambda i, j: (i, j, 0)),
                  pl.BlockSpec((1, d), lambda i, j: (0, 0))],
        out_specs=pl.BlockSpec((None, ts, d), lambda i, j: (i, j, 0)),
        out_shape=jax.ShapeDtypeStruct((b, s, d), F32),
        compiler_params=_params("arbitrary", "arbitrary"),
        name="final_norm",
    )(x, gain.reshape(1, d))


def _pack_bf16_pairs(x):
    half = x.shape[1] // 2
    lo = pltpu.bitcast(x[:, :half].astype(BF16).astype(F32), U32) >> 16
    hi = pltpu.bitcast(x[:, half:].astype(BF16).astype(F32), U32) & jnp.uint32(0xFFFF0000)
    return hi | lo


def _unpack_slab(w):
    return pltpu.bitcast(w << 16, F32), pltpu.bitcast(w & jnp.uint32(0xFFFF0000), F32)


def _store_slabs(ref, w):
    rows = w.shape[0]
    n = w.shape[1] // LANES
    for c in range(n):
        ref[pl.ds(c, rows, stride=n), :] = w[:, c * LANES:(c + 1) * LANES]


def _router_kernel(x_ref, mb_ref, tb_ref, rw_ref, tri_ref, hn_ref, comb_ref, rank_ref, count_ref,
                   *, j_shift, n_experts):
    @pl.when(jnp.logical_and(pl.program_id(0) == 0, pl.program_id(1) == 0))
    def _():
        count_ref[...] = jnp.zeros_like(count_ref)

    xn = _normed(x_ref[...])
    shift = _mod_rows(mb_ref, tb_ref, j_shift)
    scale = _mod_rows(mb_ref, tb_ref, j_shift + 1)
    hn = xn * (1.0 + scale) + shift
    _store_slabs(hn_ref, _pack_bf16_pairs(hn))
    h_hi = hn.astype(BF16)
    h_lo = (hn - h_hi.astype(F32)).astype(BF16)
    rw = rw_ref[...]
    w_hi = rw.astype(BF16)
    w_lo = (rw - w_hi.astype(F32)).astype(BF16)
    logits = _dot(h_hi, w_hi) + (_dot(h_lo, w_hi) + _dot(h_hi, w_lo))
    lane = lax.broadcasted_iota(jnp.int32, logits.shape, 1)
    neg = jnp.float32(-jnp.inf)
    l1 = jnp.where(lane < n_experts, logits, neg)
    m1 = jnp.max(l1, axis=-1, keepdims=True)
    i1 = jnp.min(jnp.where(l1 == m1, lane, LANES), axis=-1, keepdims=True)
    sel1 = lane == i1
    l2 = jnp.where(sel1, neg, l1)
    m2 = jnp.max(l2, axis=-1, keepdims=True)
    i2 = jnp.min(jnp.where(l2 == m2, lane, LANES), axis=-1, keepdims=True)
    sel2 = lane == i2
    e2 = jnp.exp(m2 - m1)
    den = 1.0 + e2
    comb_ref[...] = jnp.where(sel1, 1.0 / den, 0.0) + jnp.where(sel2, e2 / den, 0.0)
    sel = jnp.logical_or(sel1, sel2)
    self32 = jnp.where(sel, 1.0, 0.0)
    prefix = _dot(tri_ref[...], self32.astype(BF16))
    rank_ref[...] = jnp.where(sel, count_ref[0:1, :] + prefix - 1.0, -1.0)
    count_ref[0:1, :] = count_ref[0:1, :] + jnp.sum(self32, axis=0, keepdims=True)


def _router(x, mod_base, table, j_shift, router_w):
    b, s, d = x.shape
    e = router_w.shape[1]
    rw = jnp.zeros((d, LANES), F32).at[:, :e].set(router_w)
    ts = _tile(s, TS_ROUTER)
    tri = (jnp.arange(ts)[:, None] >= jnp.arange(ts)[None, :]).astype(BF16)
    n_slab = d // (2 * LANES)
    lane_out = pl.BlockSpec((None, ts, LANES), lambda i, j: (i, j, 0))
    return pl.pallas_call(
        functools.partial(_router_kernel, j_shift=j_shift, n_experts=e),
        grid=(b, s // ts),
        in_specs=[pl.BlockSpec((None, ts, d), lambda i, j: (i, j, 0)),
                  pl.BlockSpec((None, N_MOD, d), lambda i, j: (i, 0, 0)),
                  pl.BlockSpec((N_MOD, d), lambda i, j: (0, 0)),
                  pl.BlockSpec((d, LANES), lambda i, j: (0, 0)),
                  pl.BlockSpec((ts, ts), lambda i, j: (0, 0))],
        out_specs=[pl.BlockSpec((None, ts * n_slab, LANES), lambda i, j: (i, j, 0)), lane_out, lane_out],
        out_shape=[jax.ShapeDtypeStruct((b, s * n_slab, LANES), U32),
                   jax.ShapeDtypeStruct((b, s, LANES), F32),
                   jax.ShapeDtypeStruct((b, s, LANES), F32)],
        scratch_shapes=[pltpu.VMEM((8, LANES), F32)],
        compiler_params=_params("arbitrary", "arbitrary"),
        name="router",
    )(x, mod_base, table, rw, tri)


def _mm_kernel(a_ref, w_ref, o_ref):
    o_ref[...] = _dot(a_ref[...], w_ref[...]).astype(o_ref.dtype)


def _matmul(a, w_stack, layer):
    m, k = a.shape
    n = w_stack.shape[2]
    tm, tn = _tile(m, TM_MM), _tile(n, TN_MM)
    return pl.pallas_call(
        _mm_kernel,
        grid=(m // tm, n // tn),
        in_specs=[pl.BlockSpec((tm, k), lambda i, j: (i, 0)),
                  pl.BlockSpec((None, k, tn), lambda i, j: (layer, 0, j))],
        out_specs=pl.BlockSpec((tm, tn), lambda i, j: (i, j)),
        out_shape=jax.ShapeDtypeStruct((m, n), BF16),
        compiler_params=_params("arbitrary", "arbitrary"),
        name="in_proj",
    )(a, w_stack)


def _mm_res_kernel(*refs, n_lhs, j_gate, nk):
    a_refs = refs[:n_lhs]
    w_refs = refs[n_lhs:2 * n_lhs]
    x_ref, mb_ref, tb_ref, o_ref = refs[2 * n_lhs:]
    if nk == 1:
        acc = _dot(a_refs[0][...], w_refs[0][...])
        for a_ref, w_ref in zip(a_refs[1:], w_refs[1:]):
            acc = acc + _dot(a_ref[...], w_ref[...])
        o_ref[...] = x_ref[...] + _mod_rows(mb_ref, tb_ref, j_gate) * acc
        return
    k = pl.program_id(2)

    @pl.when(k == 0)
    def _():
        o_ref[...] = jnp.zeros_like(o_ref)

    for a_ref, w_ref in zip(a_refs, w_refs):
        o_ref[...] += _dot(a_ref[...], w_ref[...])

    @pl.when(k == pl.num_programs(2) - 1)
    def _():
        o_ref[...] = x_ref[...] + _mod_rows(mb_ref, tb_ref, j_gate) * o_ref[...]


def _matmul_residual(lhs, w_stacks, layer, x, mod_base, table, j_gate, seq, tn_pref, tk_pref, name):
    m, n = x.shape
    tm, tn = _tile(seq, TM_MM), _tile(n, tn_pref)
    nk = max(1, sum(a.shape[1] for a in lhs) // tk_pref)
    assert all(a.shape[1] % nk == 0 for a in lhs)
    in_specs = [pl.BlockSpec((tm, a.shape[1] // nk), lambda i, j, k: (i, k)) for a in lhs]
    in_specs += [pl.BlockSpec((None, w.shape[1] // nk, tn), lambda i, j, k: (layer, k, j)) for w in w_stacks]
    in_specs += [pl.BlockSpec((tm, tn), lambda i, j, k: (i, j)),
                 pl.BlockSpec((None, N_MOD, tn), lambda i, j, k: ((i * tm) // seq, 0, j)),
                 pl.BlockSpec((N_MOD, tn), lambda i, j, k: (0, j))]
    return pl.pallas_call(
        functools.partial(_mm_res_kernel, n_lhs=len(lhs), j_gate=j_gate, nk=nk),
        grid=(m // tm, n // tn, nk),
        in_specs=in_specs,
        out_specs=pl.BlockSpec((tm, tn), lambda i, j, k: (i, j)),
        out_shape=jax.ShapeDtypeStruct((m, n), F32),
        compiler_params=_params("arbitrary", "arbitrary", "arbitrary"),
        name=name,
    )(*lhs, *w_stacks, x, mod_base, table)


def _swiglu_kernel(a_ref, wg_ref, wu_ref, o_ref):
    a = a_ref[...]
    g = _dot(a, wg_ref[...])
    u = _dot(a, wu_ref[...])
    o_ref[...] = (_silu(g) * u).astype(o_ref.dtype)


def _swiglu_up(a, wg_stack, wu_stack, idx):
    m, k = a.shape
    n = wg_stack.shape[2]
    tm, tn = _tile(m, TM_MM), _tile(n, TN_FFN)
    wspec = pl.BlockSpec((None, k, tn), lambda i, j: (idx, 0, j))
    return pl.pallas_call(
        _swiglu_kernel,
        grid=(m // tm, n // tn),
        in_specs=[pl.BlockSpec((tm, k), lambda i, j: (i, 0)), wspec, wspec],
        out_specs=pl.BlockSpec((tm, tn), lambda i, j: (i, j)),
        out_shape=jax.ShapeDtypeStruct((m, n), BF16),
        compiler_params=_params("arbitrary", "arbitrary"),
        name="ffn_up",
    )(a, wg_stack, wu_stack)


def _retention_kernel(lg_ref, q_ref, k_ref, v_ref, g_ref, cos_ref, sin_ref, gain_ref, o_ref, state_ref,
                      *, n_chunks, chunk, dk):
    @pl.when(pl.program_id(2) == 0)
    def _():
        state_ref[...] = jnp.zeros_like(state_ref)

    lg = lg_ref[0:1, 0:1]
    row = lax.broadcasted_iota(jnp.int32, (chunk, chunk), 0)
    col = lax.broadcasted_iota(jnp.int32, (chunk, chunk), 1)
    rel = (row - col).astype(F32)
    decay = jnp.where(rel >= 0, jnp.exp(lg * jnp.maximum(rel, 0.0)), 0.0)
    ri = lax.broadcasted_iota(jnp.int32, (chunk, 1), 0).astype(F32)
    q_decay = jnp.exp(lg * (ri + 1.0))
    k_decay = jnp.exp(lg * (chunk - 1.0 - ri))
    chunk_decay = jnp.exp(lg * float(chunk))
    gain = gain_ref[...]
    k_scale = dk ** -0.5

    for c in range(n_chunks):
        rows = slice(c * chunk, (c + 1) * chunk)
        cos, sin = cos_ref[rows, :], sin_ref[rows, :]
        q = q_ref[rows, :].astype(F32)
        k = k_ref[rows, :].astype(F32)
        q = q * cos + pltpu.roll(q, dk // 2, axis=1) * sin
        k = (k * cos + pltpu.roll(k, dk // 2, axis=1) * sin) * k_scale
        qb = q.astype(BF16)
        vb = v_ref[rows, :]
        scores = lax.dot_general(qb, k.astype(BF16), (((1,), (1,)), ((), ())),
                                 preferred_element_type=F32) * decay
        state = state_ref[...]
        y = _dot(scores.astype(BF16), vb) + _dot(qb, state.astype(BF16)) * q_decay
        kd = (k * k_decay).astype(BF16)
        state_ref[...] = state * chunk_decay + lax.dot_general(
            kd, vb, (((0,), (0,)), ((), ())), preferred_element_type=F32)
        yc = y - jnp.mean(y, axis=-1, keepdims=True)
        yn = yc * lax.rsqrt(jnp.mean(yc * yc, axis=-1, keepdims=True) + EPS)
        g = g_ref[rows, :].astype(F32)
        o_ref[rows, :] = (_silu(g) * (yn * gain)).astype(o_ref.dtype)


def _retention(proj, cos2, sin2, gain, dk, dv):
    b, s, _ = proj.shape
    h = RET_HEADS
    r = _tile(s, R_RET)
    assert r % RET_CHUNK == 0 and (h * dk) % dv == 0
    log_gamma = jnp.log1p(-jnp.exp2(-5.0 - jnp.arange(h, dtype=F32)))
    lg = jnp.broadcast_to(log_gamma[:, None, None], (h, 8, LANES))
    k_blk, v_blk, g_blk = h, 2 * h * dk // dv, 2 * h * dk // dv + h
    return pl.pallas_call(
        functools.partial(_retention_kernel, n_chunks=r // RET_CHUNK, chunk=RET_CHUNK, dk=dk),
        grid=(b, h, s // r),
        in_specs=[pl.BlockSpec((None, 8, LANES), lambda i, j, t: (j, 0, 0)),
                  pl.BlockSpec((None, r, dk), lambda i, j, t: (i, t, j)),
                  pl.BlockSpec((None, r, dk), lambda i, j, t: (i, t, k_blk + j)),
                  pl.BlockSpec((None, r, dv), lambda i, j, t: (i, t, v_blk + j)),
                  pl.BlockSpec((None, r, dv), lambda i, j, t: (i, t, g_blk + j)),
                  pl.BlockSpec((None, r, dk), lambda i, j, t: (i, t, 0)),
                  pl.BlockSpec((None, r, dk), lambda i, j, t: (i, t, 0)),
                  pl.BlockSpec((1, dv), lambda i, j, t: (0, j))],
        out_specs=pl.BlockSpec((None, r, dv), lambda i, j, t: (i, t, j)),
        out_shape=jax.ShapeDtypeStruct((b, s, h * dv), BF16),
        scratch_shapes=[pltpu.VMEM((dk, dv), F32)],
        compiler_params=_params("arbitrary", "arbitrary", "arbitrary"),
        name="retention",
    )(lg, proj, proj, proj, proj, cos2, sin2, gain.reshape(1, h * dv))


def _gelu_tanh(x):
    return 0.5 * x * (1.0 + jnp.tanh(0.7978845608028654 * (x + 0.044715 * (x * x * x))))


def _softplus(z):
    return jnp.maximum(z, 0.0) + jnp.log1p(jnp.exp(-jnp.abs(z)))


def _scan_rows(a, u):
    n = a.shape[0]
    row = lax.broadcasted_iota(jnp.int32, a.shape, 0)
    d = 1
    while d < n:
        keep = row >= d
        a_prev = pltpu.roll(a, d, axis=0)
        u_prev = pltpu.roll(u, d, axis=0)
        u = jnp.where(keep, a * u_prev, 0.0) + u
        a = jnp.where(keep, a * a_prev, a)
        d *= 2
    return a, u


def _lru_kernel(x_ref, gate_ref, cw_ref, cb_ref, wcat_ref, ba_ref, bx_ref, lam_ref, gn_ref, o_ref,
                xs_ref, hs_ref, hn_ref, hprev_ref, xtail_ref, *, n_blocks, bw, conv_width, seg, pitch):
    rows = x_ref.shape[0]
    width = x_ref.shape[1]
    halo = 8
    assert rows == 8 * seg and pitch == seg + halo and conv_width - 1 <= halo

    @pl.when(pl.program_id(1) == 0)
    def _():
        xtail_ref[...] = jnp.zeros_like(xtail_ref)
        hprev_ref[...] = jnp.zeros_like(hprev_ref)

    sub = lax.broadcasted_iota(jnp.int32, (8, bw), 0)
    ss = [jnp.zeros((8, 1), F32) for _ in range(seg)]
    for n in range(n_blocks):
        cols = slice(n * bw, (n + 1) * bw)
        slab = xs_ref.at[n]
        xf = x_ref[:, cols].astype(F32)
        slab[0:halo, :] = xtail_ref[:, cols]
        for s in range(8):
            slab[s * pitch + halo:s * pitch + halo + seg, :] = xf[s * seg:(s + 1) * seg, :]
            if s > 0:
                slab[s * pitch:s * pitch + halo, :] = xf[s * seg - halo:s * seg, :]
        xtail_ref[:, cols] = xf[rows - halo:rows, :]
        first = halo - (conv_width - 1)
        taps = [slab[pl.ds(first + o, 8, stride=pitch), :] for o in range(seg + conv_width - 1)]
        cw = [cw_ref[j:j + 1, cols] for j in range(conv_width)]
        cb = cb_ref[:, cols]
        xc_rows = []
        for g in range(seg):
            acc = cb + taps[g] * cw[0]
            for j in range(1, conv_width):
                acc = acc + taps[g + j] * cw[j]
            xc_rows.append(acc)
        xc = jnp.concatenate(xc_rows, axis=0)
        gates = _dot(xc.astype(BF16), wcat_ref[n])
        r = jax.nn.sigmoid(gates[:, :bw] + ba_ref[:, cols])
        i = jax.nn.sigmoid(gates[:, bw:] + bx_ref[:, cols])
        log_a = (-LRU_C) * r * _softplus(-lam_ref[:, cols])
        a = jnp.exp(log_a)
        u = jnp.sqrt(jnp.maximum(1.0 - a * a, 0.0)) * (i * xc)
        h_loc = [u[0:8, :]]
        a_cum = [a[0:8, :]]
        for g in range(1, seg):
            a_g = a[g * 8:(g + 1) * 8, :]
            h_loc.append(a_g * h_loc[-1] + u[g * 8:(g + 1) * 8, :])
            a_cum.append(a_g * a_cum[-1])
        h_in = jnp.broadcast_to(hprev_ref[0:1, cols], (8, bw))
        f_prev = jnp.where(sub == 0, h_in, pltpu.roll(h_loc[-1], 1, axis=0))
        p_prev = jnp.where(sub == 0, 0.0, pltpu.roll(a_cum[-1], 1, axis=0))
        _, carry = _scan_rows(p_prev, f_prev)
        hprev_ref[0:1, cols] = (h_loc[-1] + a_cum[-1] * carry)[7:8, :]
        slab_h = hs_ref.at[n]
        for g in range(seg):
            h_g = h_loc[g] + a_cum[g] * carry
            slab_h[g * 8:(g + 1) * 8, :] = h_g
            ss[g] = ss[g] + jnp.sum(h_g * h_g, axis=-1, keepdims=True)
    inv = [lax.rsqrt(ss_g / width + EPS) for ss_g in ss]
    for n in range(n_blocks):
        cols = slice(n * bw, (n + 1) * bw)
        slab_h = hs_ref.at[n]
        gn = gn_ref[:, cols]
        for g in range(seg):
            hn_ref[pl.ds(g, 8, stride=pitch), :] = slab_h[g * 8:(g + 1) * 8, :] * inv[g] * gn
        for s in range(8):
            t_rows = slice(s * seg, (s + 1) * seg)
            gate = gate_ref[t_rows, cols].astype(F32)
            o_ref[t_rows, cols] = (_gelu_tanh(gate) * hn_ref[s * pitch:s * pitch + seg, :]).astype(o_ref.dtype)


def _rg_lru(proj, gate_off, x_off, conv_w, conv_b, wcat, b_a, b_x, lam, gnorm):
    b, s, _ = proj.shape
    n_blocks, bw, _ = wcat.shape
    w = n_blocks * bw
    assert gate_off % w == 0 and x_off % w == 0
    r = _tile(s, R_LRU)
    seg = r // 8
    pitch = seg + 8
    assert r % 128 == 0 and bw == LANES
    conv_width = conv_w.shape[0]
    vec = pl.BlockSpec((1, w), lambda i, t: (0, 0))
    return pl.pallas_call(
        functools.partial(_lru_kernel, n_blocks=n_blocks, bw=bw, conv_width=conv_width, seg=seg, pitch=pitch),
        grid=(b, s // r),
        in_specs=[pl.BlockSpec((None, r, w), lambda i, t: (i, t, x_off // w)),
                  pl.BlockSpec((None, r, w), lambda i, t: (i, t, gate_off // w)),
                  pl.BlockSpec((conv_width, w), lambda i, t: (0, 0)),
                  vec,
                  pl.BlockSpec((n_blocks, bw, 2 * bw), lambda i, t: (0, 0, 0)),
                  vec, vec, vec, vec],
        out_specs=pl.BlockSpec((None, r, w), lambda i, t: (i, t, 0)),
        out_shape=jax.ShapeDtypeStruct((b, s, w), BF16),
        scratch_shapes=[pltpu.VMEM((n_blocks, 8 * pitch, bw), F32), pltpu.VMEM((n_blocks, r, bw), F32),
                        pltpu.VMEM((8 * pitch, bw), F32), pltpu.VMEM((8, w), F32), pltpu.VMEM((8, w), F32)],
        compiler_params=_params("arbitrary", "arbitrary"),
        name="rg_lru",
    )(proj, proj, conv_w, conv_b.reshape(1, w), wcat, b_a.reshape(1, w), b_x.reshape(1, w),
      lam.reshape(1, w), gnorm.reshape(1, w))


def _dispatch_kernel(pos_ref, pad_lo_ref, pad_hi_ref, nu_ref, src_ref, xs_ref, zero_ref, sem,
                     *, lag, n_tok, n_experts, tm, n_tiles):
    step = pl.program_id(0)

    def row_copy(t, k):
        return pltpu.make_async_copy(src_ref.at[pl.ds(t - step * lag, 1)],
                                     xs_ref.at[pl.ds(pos_ref[k * n_tok + t], 1)], sem.at[0])

    def start(t, carry):
        row_copy(t, 0).start()
        row_copy(t, 1).start()
        return carry

    def wait(t, carry):
        row_copy(t, 0).wait()
        row_copy(t, 1).wait()
        return carry

    lax.fori_loop(step * lag, (step + 1) * lag, start, 0)

    @pl.when(step == 0)
    def _():
        zero_ref[...] = jnp.zeros_like(zero_ref)

        def pad_copy(p):
            return pltpu.make_async_copy(zero_ref.at[pl.ds(0, 1)], xs_ref.at[pl.ds(p, 1)], sem.at[1])

        def tile_copy(t):
            return pltpu.make_async_copy(zero_ref, xs_ref.at[pl.ds(t * tm, tm)], sem.at[1])

        def run(lo, hi, make):
            def go(p, carry):
                make(p).start()
                return carry

            def done(p, carry):
                make(p).wait()
                return carry

            lax.fori_loop(lo, hi, go, 0)
            lax.fori_loop(lo, hi, done, 0)

        for e in range(n_experts):
            run(pad_lo_ref[e], pad_hi_ref[e], pad_copy)
        run(nu_ref[0], n_tiles, tile_copy)

    lax.fori_loop(step * lag, (step + 1) * lag, wait, 0)


def _dispatch(hn_packed, pos, pad_lo, pad_hi, n_used, tm, n_rows):
    t, n_slab, _ = hn_packed.shape
    n_experts = pad_lo.shape[0]
    lag = _tile(t, DISPATCH_LAG)
    return pl.pallas_call(
        functools.partial(_dispatch_kernel, lag=lag, n_tok=t, n_experts=n_experts, tm=tm, n_tiles=n_rows // tm),
        grid_spec=pltpu.PrefetchScalarGridSpec(
            num_scalar_prefetch=4,
            grid=(t // lag,),
            in_specs=[pl.BlockSpec((lag, n_slab, LANES), lambda i, *_: (i, 0, 0))],
            out_specs=pl.BlockSpec(memory_space=pl.ANY),
            scratch_shapes=[pltpu.VMEM((tm, n_slab, LANES), U32), pltpu.SemaphoreType.DMA((2,))]),
        out_shape=jax.ShapeDtypeStruct((n_rows, n_slab, LANES), U32),
        compiler_params=_params("arbitrary"),
        name="moe_dispatch",
    )(pos, pad_lo, pad_hi, n_used, hn_packed)


def _moe_ffn_kernel(te_ref, nu_ref, xs_ref, wg_ref, wu_ref, wdlo_ref, wdhi_ref, y_ref, abf_ref, h_ref):
    j = pl.program_id(1)
    used = pl.program_id(0) < nu_ref[0]
    tm = abf_ref.shape[0]
    n_up = h_ref.shape[0]
    n_slab = y_ref.shape[0] // tm
    half = n_slab * LANES
    slabs_per_step = wdlo_ref.shape[1] // LANES

    @pl.when(jnp.logical_and(used, j == 0))
    def _():
        for c in range(n_slab):
            lo, hi = _unpack_slab(xs_ref[pl.ds(c, tm, stride=n_slab), :])
            abf_ref[:, c * LANES:(c + 1) * LANES] = lo.astype(BF16)
            abf_ref[:, half + c * LANES:half + (c + 1) * LANES] = hi.astype(BF16)

    @pl.when(jnp.logical_and(used, j < n_up))
    def _():
        a = abf_ref[...]
        g = _dot(a, wg_ref[...])
        u = _dot(a, wu_ref[...])
        h_ref[j] = (_silu(g) * u).astype(BF16)

    @pl.when(jnp.logical_and(used, j >= n_up))
    def _():
        h = jnp.concatenate([h_ref[c] for c in range(n_up)], axis=1)
        lo = pltpu.bitcast(_dot(h, wdlo_ref[...]).astype(BF16).astype(F32), U32) >> 16
        hi = pltpu.bitcast(_dot(h, wdhi_ref[...]).astype(BF16).astype(F32), U32) & jnp.uint32(0xFFFF0000)
        packed = hi | lo
        first = (j - n_up) * slabs_per_step
        for c in range(slabs_per_step):
            y_ref[pl.ds(first + c, tm, stride=n_slab), :] = packed[:, c * LANES:(c + 1) * LANES]

    @pl.when(jnp.logical_and(jnp.logical_not(used), j == 0))
    def _():
        y_ref[...] = jnp.zeros_like(y_ref)


def _moe_ffn(xs, wg_stack, wu_stack, wd_stack, layer_idx, tile_expert, n_used, tm):
    p, n_slab, _ = xs.shape
    d = 2 * n_slab * LANES
    de = wg_stack.shape[3]
    tn = _tile(de, TN_MOE_UP)
    n_up = de // tn
    cw = _tile(d // 2, TN_MOE_DOWN)
    n_down = (d // 2) // cw
    row_in = pl.BlockSpec((tm * n_slab, LANES), lambda i, j, te, nu: (jnp.minimum(i, nu[0] - 1), 0))
    w_up = pl.BlockSpec((None, None, d, tn),
                        lambda i, j, te, nu: (layer_idx, te[i], 0, jnp.minimum(j, n_up - 1)))
    w_down_lo = pl.BlockSpec((None, None, de, cw),
                             lambda i, j, te, nu: (layer_idx, te[i], 0, jnp.maximum(j - n_up, 0)))
    w_down_hi = pl.BlockSpec((None, None, de, cw),
                             lambda i, j, te, nu: (layer_idx, te[i], 0, n_down + jnp.maximum(j - n_up, 0)))
    return pl.pallas_call(
        _moe_ffn_kernel,
        grid_spec=pltpu.PrefetchScalarGridSpec(
            num_scalar_prefetch=2,
            grid=(p // tm, n_up + n_down),
            in_specs=[row_in, w_up, w_up, w_down_lo, w_down_hi],
            out_specs=pl.BlockSpec((tm * n_slab, LANES), lambda i, j, te, nu: (i, 0)),
            scratch_shapes=[pltpu.VMEM((tm, d), BF16), pltpu.VMEM((n_up, tm, tn), BF16)]),
        out_shape=jax.ShapeDtypeStruct((p * n_slab, LANES), U32),
        compiler_params=_params("arbitrary", "arbitrary"),
        name="moe_ffn",
    )(tile_expert, n_used, xs.reshape(p * n_slab, LANES), wg_stack, wu_stack, wd_stack, wd_stack)


def _combine_kernel(pos_ref, y_ref, x_ref, w_ref, mb_ref, tb_ref, o_ref, buf_ref, sem, *, g, n_tok, j_gate):
    step = pl.program_id(0)
    n_slab = buf_ref.shape[2] // g

    def row_copies(base, r, slot):
        dst = pl.multiple_of(r * n_slab, n_slab)
        for k in range(2):
            src = pl.multiple_of(pos_ref[k * n_tok + base + r] * n_slab, n_slab)
            yield pltpu.make_async_copy(y_ref.at[pl.ds(src, n_slab)], buf_ref.at[slot, k, pl.ds(dst, n_slab)],
                                        sem.at[slot])

    def start_row(r, carry):
        for cp in row_copies(0, r, 0):
            cp.start()
        return carry

    @pl.when(step == 0)
    def _():
        lax.fori_loop(0, g, start_row, 0)

    slot = step % 2

    def wait_row(r, carry):
        for cp in row_copies(step * g, r, slot):
            cp.wait()
        return carry

    lax.fori_loop(0, g, wait_row, 0)

    gate = _mod_rows(mb_ref, tb_ref, j_gate)
    w_a, w_b = w_ref[:, 0:1], w_ref[:, 1:2]
    half = n_slab * LANES
    rows_a, rows_b = buf_ref.at[slot, 0], buf_ref.at[slot, 1]

    def start_next_row(r, carry):
        for cp in row_copies((step + 1) * g, r, 1 - slot):
            cp.start()
        return carry

    @pl.when(step + 1 < pl.num_programs(0))
    def _():
        lax.fori_loop(0, g, start_next_row, 0)

    for c in range(n_slab):
        a_lo, a_hi = _unpack_slab(rows_a[pl.ds(c, g, stride=n_slab), :])
        b_lo, b_hi = _unpack_slab(rows_b[pl.ds(c, g, stride=n_slab), :])
        lo = slice(c * LANES, (c + 1) * LANES)
        hi = slice(half + c * LANES, half + (c + 1) * LANES)
        o_ref[:, lo] = x_ref[:, lo] + gate[:, lo] * (w_a * a_lo + w_b * b_lo)
        o_ref[:, hi] = x_ref[:, hi] + gate[:, hi] * (w_a * a_hi + w_b * b_hi)


def _moe_combine(y, pos, w2, x, mod_base, table, j_gate, seq):
    t, d = x.shape
    n_slab = d // (2 * LANES)
    g = _tile(seq, G_ROWS)
    return pl.pallas_call(
        functools.partial(_combine_kernel, g=g, n_tok=t, j_gate=j_gate),
        grid_spec=pltpu.PrefetchScalarGridSpec(
            num_scalar_prefetch=1,
            grid=(t // g,),
            in_specs=[pl.BlockSpec(memory_space=pl.ANY),
                      pl.BlockSpec((g, d), lambda i, pos_ref: (i, 0)),
                      pl.BlockSpec((g, 2), lambda i, pos_ref: (i, 0)),
                      pl.BlockSpec((None, N_MOD, d), lambda i, pos_ref: ((i * g) // seq, 0, 0)),
                      pl.BlockSpec((N_MOD, d), lambda i, pos_ref: (0, 0))],
            out_specs=pl.BlockSpec((g, d), lambda i, pos_ref: (i, 0)),
            scratch_shapes=[pltpu.VMEM((2, 2, g * n_slab, LANES), U32), pltpu.SemaphoreType.DMA((2,))]),
        out_shape=jax.ShapeDtypeStruct((t, d), F32),
        compiler_params=_params("arbitrary"),
        name="moe_combine",
    )(pos, y, x, w2, mod_base, table)


def _route_plan(rank, comb, tm):
    t, e = rank.shape
    sel = rank >= 0.0
    counts = jnp.sum(sel.astype(jnp.int32), axis=0)
    padded = ((counts + tm - 1) // tm) * tm
    ends = jnp.cumsum(padded)
    starts = ends - padded
    dest = starts[None, :] + rank.astype(jnp.int32)
    pos_a = jnp.min(jnp.where(sel, dest, jnp.int32(2 ** 30)), axis=1)
    pos_b = jnp.max(jnp.where(sel, dest, jnp.int32(-1)), axis=1)
    w_a = jnp.sum(jnp.where(jnp.logical_and(sel, dest == pos_a[:, None]), comb, 0.0), axis=1)
    w_b = jnp.sum(jnp.where(jnp.logical_and(sel, dest == pos_b[:, None]), comb, 0.0), axis=1)
    pos = jnp.concatenate([pos_a, pos_b]).astype(jnp.int32)
    w2 = jnp.stack([w_a, w_b], axis=1)
    n_rows = 2 * t + e * tm
    n_tiles = n_rows // tm
    tile_start = jnp.arange(n_tiles, dtype=jnp.int32) * tm
    n_used = (ends[-1] // tm).astype(jnp.int32)
    te_raw = jnp.sum((tile_start[:, None] >= ends[None, :]).astype(jnp.int32), axis=1)
    last_e = jnp.sum((ends[-1] - 1 >= ends).astype(jnp.int32))
    tile_expert = jnp.where(tile_start < ends[-1], jnp.minimum(te_raw, e - 1), last_e).astype(jnp.int32)
    return (pos, w2, tile_expert, n_used.reshape(1), (starts + counts).astype(jnp.int32),
            ends.astype(jnp.int32), n_rows)


def kernel(x, c, positions, ada_w, ada_b, ada_table, w_in, conv_w, conv_b, gate_a_w, gate_a_b, gate_x_w,
           gate_x_b, lru_lambda, ret_norm_g, lru_norm_g, w_out, ffn_w_gate, ffn_w_up, ffn_w_down, router_w,
           moe_w_gate, moe_w_up, moe_w_down, final_norm_g):
    b, s, d = x.shape
    depth = w_in.shape[0]
    t = b * s
    ret_v = ret_norm_g.shape[1]
    lru_w = lru_norm_g.shape[1]
    ret_qk = (w_in.shape[2] - 2 * ret_v - 2 * lru_w) // 2
    dk, dv = ret_qk // RET_HEADS, ret_v // RET_HEADS
    gate_off = 2 * ret_qk + 2 * ret_v
    x_off = gate_off + lru_w
    n_experts = router_w.shape[2]
    tm_moe = _tile(t, TM_MOE)

    w_in_b = w_in.astype(BF16)
    w_out_ret = w_out[:, :ret_v].astype(BF16)
    w_out_lru = w_out[:, ret_v:].astype(BF16)
    wcat = jnp.concatenate([gate_a_w, gate_x_w], axis=-1).astype(BF16)
    ffn_g, ffn_u, ffn_d = ffn_w_gate.astype(BF16), ffn_w_up.astype(BF16), ffn_w_down.astype(BF16)
    moe_g, moe_u, moe_d = moe_w_gate.astype(BF16), moe_w_up.astype(BF16), moe_w_down.astype(BF16)

    mod_base = _ada_base(c, ada_w, ada_b)
    cos2, sin2 = _rope_tables(positions, dk)

    for l in range(depth):
        table = ada_table[l]
        hn = _norm_mod(x, mod_base, table, 0)
        proj = _matmul(hn.reshape(t, d), w_in_b, l).reshape(b, s, -1)
        y_ret = _retention(proj, cos2, sin2, ret_norm_g[l], dk, dv)
        y_lru = _rg_lru(proj, gate_off, x_off, conv_w[l], conv_b[l], wcat[l], gate_a_b[l], gate_x_b[l],
                        lru_lambda[l], lru_norm_g[l])
        x = _matmul_residual([y_ret.reshape(t, ret_v), y_lru.reshape(t, lru_w)], [w_out_ret, w_out_lru], l,
                             x.reshape(t, d), mod_base, table, 2, s, TN_OUT, TK_OUT, "out_proj").reshape(b, s, d)
        if l % 2 == 0:
            hn = _norm_mod(x, mod_base, table, 3)
            hmid = _swiglu_up(hn.reshape(t, d), ffn_g, ffn_u, l // 2)
            x = _matmul_residual([hmid], [ffn_d], l // 2, x.reshape(t, d), mod_base, table, 5, s, TN_RES, TK_RES,
                                 "ffn_down").reshape(b, s, d)
        else:
            hn_packed, comb, rank = _router(x, mod_base, table, 3, router_w[l // 2])
            comb = comb.reshape(t, LANES)[:, :n_experts]
            rank = rank.reshape(t, LANES)[:, :n_experts]
            pos, w2, tile_expert, n_used, pad_lo, pad_hi, n_rows = _route_plan(rank, comb, tm_moe)
            xs = _dispatch(hn_packed.reshape(t, -1, LANES), pos, pad_lo, pad_hi, n_used, tm_moe, n_rows)
            y = _moe_ffn(xs, moe_g, moe_u, moe_d, l // 2, tile_expert, n_used, tm_moe)
            x = _moe_combine(y, pos, w2, x.reshape(t, d), mod_base, table, 5, s).reshape(b, s, d)
    return _final_norm(x, final_norm_g)
```

```python
import functools

import jax
import jax.numpy as jnp
from jax import lax
from jax.experimental import pallas as pl
from jax.experimental.pallas import tpu as pltpu

F32 = jnp.float32
BF16 = jnp.bfloat16
U32 = jnp.uint32

RET_HEADS = 8
RET_CHUNK = 128
ROPE_BASE = 10000.0
LRU_C = 8.0
N_MOD = 6
EPS = 1e-6

VMEM_LIMIT_BYTES = 56 * 1024 * 1024
LANES = 128

TM_MM = 1024
TN_MM = 1024
TN_RES = 1024
TK_RES = 2048
TN_OUT = 512
TK_OUT = 4096
TN_FFN = 512
TS_NORM = 512
TS_ROUTER = 256
R_RET = 2048
R_LRU = 256
TM_MOE = 512
TN_MOE_UP = 256
TN_MOE_DOWN = 1024
DISPATCH_LAG = 1024
G_ROWS = 256
TN_ADA = 512


def _tile(dim, pref):
    if dim <= pref:
        return dim
    for t in range(pref, 0, -1):
        if dim % t == 0 and t % LANES == 0:
            return t
    for t in range(pref, 0, -1):
        if dim % t == 0 and t % 8 == 0:
            return t
    return dim


def _params(*sem):
    return pltpu.CompilerParams(dimension_semantics=sem, vmem_limit_bytes=VMEM_LIMIT_BYTES)


def _silu(x):
    return x * jax.nn.sigmoid(x)


def _dot(a, b):
    return jnp.dot(a, b, preferred_element_type=F32)


def _ada_kernel(c_ref, w_ref, b_ref, o_ref):
    c = c_ref[...]
    o_ref[...] = _dot(_silu(c).astype(BF16), w_ref[...].astype(BF16)) + b_ref[...]


def _ada_base(c, ada_w, ada_b):
    b, d = c.shape
    n = ada_w.shape[1]
    rows = 8
    cp = jnp.zeros((rows, d), F32).at[:b].set(c)
    tn = _tile(n, TN_ADA)
    out = pl.pallas_call(
        _ada_kernel,
        grid=(n // tn,),
        in_specs=[pl.BlockSpec((rows, d), lambda j: (0, 0)),
                  pl.BlockSpec((d, tn), lambda j: (0, j)),
                  pl.BlockSpec((1, tn), lambda j: (0, j))],
        out_specs=pl.BlockSpec((rows, tn), lambda j: (0, j)),
        out_shape=jax.ShapeDtypeStruct((rows, n), F32),
        compiler_params=_params("arbitrary"),
        name="ada_base",
    )(cp, ada_w, ada_b.reshape(1, n))
    return out[:b].reshape(b, N_MOD, d)


def _rope_kernel(pos_ref, inv_ref, sign_ref, cos_ref, sin_ref):
    ang = pos_ref[...].astype(F32) * inv_ref[...]
    cos_ref[...] = jnp.cos(ang)
    sin_ref[...] = jnp.sin(ang) * sign_ref[...]


def _rope_tables(positions, dk):
    b, s = positions.shape
    inv = jnp.exp2(-jnp.arange(0, dk, 2, dtype=F32) / dk * jnp.log2(jnp.float32(ROPE_BASE)))
    inv2 = jnp.concatenate([inv, inv]).reshape(1, dk)
    sign = jnp.concatenate([-jnp.ones((dk // 2,), F32), jnp.ones((dk // 2,), F32)]).reshape(1, dk)
    ts = _tile(s, 1024)
    spec = pl.BlockSpec((None, ts, dk), lambda i, j: (i, j, 0))
    return pl.pallas_call(
        _rope_kernel,
        grid=(b, s // ts),
        in_specs=[pl.BlockSpec((None, ts, 1), lambda i, j: (i, j, 0)),
                  pl.BlockSpec((1, dk), lambda i, j: (0, 0)),
                  pl.BlockSpec((1, dk), lambda i, j: (0, 0))],
        out_specs=[spec, spec],
        out_shape=[jax.ShapeDtypeStruct((b, s, dk), F32)] * 2,
        compiler_params=_params("arbitrary", "arbitrary"),
        name="rope_tables",
    )(positions.reshape(b, s, 1), inv2, sign)


def _mod_rows(mb_ref, tb_ref, j):
    return mb_ref[j:j + 1, :] + tb_ref[j:j + 1, :]


def _normed(x):
    return x * lax.rsqrt(jnp.mean(x * x, axis=-1, keepdims=True) + EPS)


def _norm_mod_kernel(x_ref, mb_ref, tb_ref, o_ref, *, j_shift):
    xn = _normed(x_ref[...])
    shift = _mod_rows(mb_ref, tb_ref, j_shift)
    scale = _mod_rows(mb_ref, tb_ref, j_shift + 1)
    o_ref[...] = (xn * (1.0 + scale) + shift).astype(o_ref.dtype)


def _norm_mod(x, mod_base, table, j_shift):
    b, s, d = x.shape
    ts = _tile(s, TS_NORM)
    return pl.pallas_call(
        functools.partial(_norm_mod_kernel, j_shift=j_shift),
        grid=(b, s // ts),
        in_specs=[pl.BlockSpec((None, ts, d), lambda i, j: (i, j, 0)),
                  pl.BlockSpec((None, N_MOD, d), lambda i, j: (i, 0, 0)),
                  pl.BlockSpec((N_MOD, d), lambda i, j: (0, 0))],
        out_specs=pl.BlockSpec((None, ts, d), lambda i, j: (i, j, 0)),
        out_shape=jax.ShapeDtypeStruct((b, s, d), BF16),
        compiler_params=_params("arbitrary", "arbitrary"),
        name="norm_mod",
    )(x, mod_base, table)


def _final_norm_kernel(x_ref, g_ref, o_ref):
    o_ref[...] = _normed(x_ref[...]) * g_ref[...]


def _final_norm(x, gain):
    b, s, d = x.shape
    ts = _tile(s, TS_NORM)
    return pl.pallas_call(
        _final_norm_kernel,
        grid=(b, s // ts),
        in_specs=[pl.BlockSpec((None, ts, d), lambda i, j: (i, j, 0)),
                  pl.BlockSpec((1, d), lambda i, j: (0, 0))],
        out_specs=pl.BlockSpec((None, ts, d), lambda i, j: (i, j, 0)),
        out_shape=jax.ShapeDtypeStruct((b, s, d), F32),
        compiler_params=_params("arbitrary", "arbitrary"),
        name="final_norm",
    )(x, gain.reshape(1, d))


def _pack_bf16_pairs(x):
    half = x.shape[1] // 2
    lo = pltpu.bitcast(x[:, :half].astype(BF16).astype(F32), U32) >> 16
    hi = pltpu.bitcast(x[:, half:].astype(BF16).astype(F32), U32) & jnp.uint32(0xFFFF0000)
    return hi | lo


def _unpack_slab(w):
    return pltpu.bitcast(w << 16, F32), pltpu.bitcast(w & jnp.uint32(0xFFFF0000), F32)


def _store_slabs(ref, w):
    rows = w.shape[0]
    n = w.shape[1] // LANES
    for c in range(n):
        ref[pl.ds(c, rows, stride=n), :] = w[:, c * LANES:(c + 1) * LANES]


def _router_kernel(x_ref, mb_ref, tb_ref, rw_ref, tri_ref, hn_ref, comb_ref, rank_ref, count_ref,
                   *, j_shift, n_experts):
    @pl.when(jnp.logical_and(pl.program_id(0) == 0, pl.program_id(1) == 0))
    def _():
        count_ref[...] = jnp.zeros_like(count_ref)

    xn = _normed(x_ref[...])
    shift = _mod_rows(mb_ref, tb_ref, j_shift)
    scale = _mod_rows(mb_ref, tb_ref, j_shift + 1)
    hn = xn * (1.0 + scale) + shift
    _store_slabs(hn_ref, _pack_bf16_pairs(hn))
    h_hi = hn.astype(BF16)
    h_lo = (hn - h_hi.astype(F32)).astype(BF16)
    rw = rw_ref[...]
    w_hi = rw.astype(BF16)
    w_lo = (rw - w_hi.astype(F32)).astype(BF16)
    logits = _dot(h_hi, w_hi) + (_dot(h_lo, w_hi) + _dot(h_hi, w_lo))
    lane = lax.broadcasted_iota(jnp.int32, logits.shape, 1)
    neg = jnp.float32(-jnp.inf)
    l1 = jnp.where(lane < n_experts, logits, neg)
    m1 = jnp.max(l1, axis=-1, keepdims=True)
    i1 = jnp.min(jnp.where(l1 == m1, lane, LANES), axis=-1, keepdims=True)
    sel1 = lane == i1
    l2 = jnp.where(sel1, neg, l1)
    m2 = jnp.max(l2, axis=-1, keepdims=True)
    i2 = jnp.min(jnp.where(l2 == m2, lane, LANES), axis=-1, keepdims=True)
    sel2 = lane == i2
    e2 = jnp.exp(m2 - m1)
    den = 1.0 + e2
    comb_ref[...] = jnp.where(sel1, 1.0 / den, 0.0) + jnp.where(sel2, e2 / den, 0.0)
    sel = jnp.logical_or(sel1, sel2)
    self32 = jnp.where(sel, 1.0, 0.0)
    prefix = _dot(tri_ref[...], self32.astype(BF16))
    rank_ref[...] = jnp.where(sel, count_ref[0:1, :] + prefix - 1.0, -1.0)
    count_ref[0:1, :] = count_ref[0:1, :] + jnp.sum(self32, axis=0, keepdims=True)


def _router(x, mod_base, table, j_shift, router_w):
    b, s, d = x.shape
    e = router_w.shape[1]
    rw = jnp.zeros((d, LANES), F32).at[:, :e].set(router_w)
    ts = _tile(s, TS_ROUTER)
    tri = (jnp.arange(ts)[:, None] >= jnp.arange(ts)[None, :]).astype(BF16)
    n_slab = d // (2 * LANES)
    lane_out = pl.BlockSpec((None, ts, LANES), lambda i, j: (i, j, 0))
    return pl.pallas_call(
        functools.partial(_router_kernel, j_shift=j_shift, n_experts=e),
        grid=(b, s // ts),
        in_specs=[pl.BlockSpec((None, ts, d), lambda i, j: (i, j, 0)),
                  pl.BlockSpec((None, N_MOD, d), lambda i, j: (i, 0, 0)),
                  pl.BlockSpec((N_MOD, d), lambda i, j: (0, 0)),
                  pl.BlockSpec((d, LANES), lambda i, j: (0, 0)),
                  pl.BlockSpec((ts, ts), lambda i, j: (0, 0))],
        out_specs=[pl.BlockSpec((None, ts * n_slab, LANES), lambda i, j: (i, j, 0)), lane_out, lane_out],
        out_shape=[jax.ShapeDtypeStruct((b, s * n_slab, LANES), U32),
                   jax.ShapeDtypeStruct((b, s, LANES), F32),
                   jax.ShapeDtypeStruct((b, s, LANES), F32)],
        scratch_shapes=[pltpu.VMEM((8, LANES), F32)],
        compiler_params=_params("arbitrary", "arbitrary"),
        name="router",
    )(x, mod_base, table, rw, tri)


def _mm_kernel(a_ref, w_ref, o_ref):
    o_ref[...] = _dot(a_ref[...], w_ref[...]).astype(o_ref.dtype)


def _matmul(a, w_stack, layer):
    m, k = a.shape
    n = w_stack.shape[2]
    tm, tn = _tile(m, TM_MM), _tile(n, TN_MM)
    return pl.pallas_call(
        _mm_kernel,
        grid=(m // tm, n // tn),
        in_specs=[pl.BlockSpec((tm, k), lambda i, j: (i, 0)),
                  pl.BlockSpec((None, k, tn), lambda i, j: (layer, 0, j))],
        out_specs=pl.BlockSpec((tm, tn), lambda i, j: (i, j)),
        out_shape=jax.ShapeDtypeStruct((m, n), BF16),
        compiler_params=_params("arbitrary", "arbitrary"),
        name="in_proj",
    )(a, w_stack)


def _mm_res_kernel(*refs, n_lhs, j_gate, nk):
    a_refs = refs[:n_lhs]
    w_refs = refs[n_lhs:2 * n_lhs]
    x_ref, mb_ref, tb_ref, o_ref = refs[2 * n_lhs:]
    if nk == 1:
        acc = _dot(a_refs[0][...], w_refs[0][...])
        for a_ref, w_ref in zip(a_refs[1:], w_refs[1:]):
            acc = acc + _dot(a_ref[...], w_ref[...])
        o_ref[...] = x_ref[...] + _mod_rows(mb_ref, tb_ref, j_gate) * acc
        return
    k = pl.program_id(2)

    @pl.when(k == 0)
    def _():
        o_ref[...] = jnp.zeros_like(o_ref)

    for a_ref, w_ref in zip(a_refs, w_refs):
        o_ref[...] += _dot(a_ref[...], w_ref[...])

    @pl.when(k == pl.num_programs(2) - 1)
    def _():
        o_ref[...] = x_ref[...] + _mod_rows(mb_ref, tb_ref, j_gate) * o_ref[...]


def _matmul_residual(lhs, w_stacks, layer, x, mod_base, table, j_gate, seq, tn_pref, tk_pref, name):
    m, n = x.shape
    tm, tn = _tile(seq, TM_MM), _tile(n, tn_pref)
    nk = max(1, sum(a.shape[1] for a in lhs) // tk_pref)
    assert all(a.shape[1] % nk == 0 for a in lhs)
    in_specs = [pl.BlockSpec((tm, a.shape[1] // nk), lambda i, j, k: (i, k)) for a in lhs]
    in_specs += [pl.BlockSpec((None, w.shape[1] // nk, tn), lambda i, j, k: (layer, k, j)) for w in w_stacks]
    in_specs += [pl.BlockSpec((tm, tn), lambda i, j, k: (i, j)),
                 pl.BlockSpec((None, N_MOD, tn), lambda i, j, k: ((i * tm) // seq, 0, j)),
                 pl.BlockSpec((N_MOD, tn), lambda i, j, k: (0, j))]
    return pl.pallas_call(
        functools.partial(_mm_res_kernel, n_lhs=len(lhs), j_gate=j_gate, nk=nk),
        grid=(m // tm, n // tn, nk),
        in_specs=in_specs,
        out_specs=pl.BlockSpec((tm, tn), lambda i, j, k: (i, j)),
        out_shape=jax.ShapeDtypeStruct((m, n), F32),
        compiler_params=_params("arbitrary", "arbitrary", "arbitrary"),
        name=name,
    )(*lhs, *w_stacks, x, mod_base, table)


def _swiglu_kernel(a_ref, wg_ref, wu_ref, o_ref):
    a = a_ref[...]
    g = _dot(a, wg_ref[...])
    u = _dot(a, wu_ref[...])
    o_ref[...] = (_silu(g) * u).astype(o_ref.dtype)


def _swiglu_up(a, wg_stack, wu_stack, idx):
    m, k = a.shape
    n = wg_stack.shape[2]
    tm, tn = _tile(m, TM_MM), _tile(n, TN_FFN)
    wspec = pl.BlockSpec((None, k, tn), lambda i, j: (idx, 0, j))
    return pl.pallas_call(
        _swiglu_kernel,
        grid=(m // tm, n // tn),
        in_specs=[pl.BlockSpec((tm, k), lambda i, j: (i, 0)), wspec, wspec],
        out_specs=pl.BlockSpec((tm, tn), lambda i, j: (i, j)),
        out_shape=jax.ShapeDtypeStruct((m, n), BF16),
        compiler_params=_params("arbitrary", "arbitrary"),
        name="ffn_up",
    )(a, wg_stack, wu_stack)


def _retention_kernel(lg_ref, q_ref, k_ref, v_ref, g_ref, cos_ref, sin_ref, gain_ref, o_ref, state_ref,
                      *, n_chunks, chunk, dk):
    @pl.when(pl.program_id(2) == 0)
    def _():
        state_ref[...] = jnp.zeros_like(state_ref)

    lg = lg_ref[0:1, 0:1]
    row = lax.broadcasted_iota(jnp.int32, (chunk, chunk), 0)
    col = lax.broadcasted_iota(jnp.int32, (chunk, chunk), 1)
    rel = (row - col).astype(F32)
    decay = jnp.where(rel >= 0, jnp.exp(lg * jnp.maximum(rel, 0.0)), 0.0)
    ri = lax.broadcasted_iota(jnp.int32, (chunk, 1), 0).astype(F32)
    q_decay = jnp.exp(lg * (ri + 1.0))
    k_decay = jnp.exp(lg * (chunk - 1.0 - ri))
    chunk_decay = jnp.exp(lg * float(chunk))
    gain = gain_ref[...]
    k_scale = dk ** -0.5

    for c in range(n_chunks):
        rows = slice(c * chunk, (c + 1) * chunk)
        cos, sin = cos_ref[rows, :], sin_ref[rows, :]
        q = q_ref[rows, :].astype(F32)
        k = k_ref[rows, :].astype(F32)
        q = q * cos + pltpu.roll(q, dk // 2, axis=1) * sin
        k = (k * cos + pltpu.roll(k, dk // 2, axis=1) * sin) * k_scale
        qb = q.astype(BF16)
        vb = v_ref[rows, :]
        scores = lax.dot_general(qb, k.astype(BF16), (((1,), (1,)), ((), ())),
                                 preferred_element_type=F32) * decay
        state = state_ref[...]
        y = _dot(scores.astype(BF16), vb) + _dot(qb, state.astype(BF16)) * q_decay
        kd = (k * k_decay).astype(BF16)
        state_ref[...] = state * chunk_decay + lax.dot_general(
            kd, vb, (((0,), (0,)), ((), ())), preferred_element_type=F32)
        yc = y - jnp.mean(y, axis=-1, keepdims=True)
        yn = yc * lax.rsqrt(jnp.mean(yc * yc, axis=-1, keepdims=True) + EPS)
        g = g_ref[rows, :].astype(F32)
        o_ref[rows, :] = (_silu(g) * (yn * gain)).astype(o_ref.dtype)


def _retention(proj, cos2, sin2, gain, dk, dv):
    b, s, _ = proj.shape
    h = RET_HEADS
    r = _tile(s, R_RET)
    assert r % RET_CHUNK == 0 and (h * dk) % dv == 0
    log_gamma = jnp.log1p(-jnp.exp2(-5.0 - jnp.arange(h, dtype=F32)))
    lg = jnp.broadcast_to(log_gamma[:, None, None], (h, 8, LANES))
    k_blk, v_blk, g_blk = h, 2 * h * dk // dv, 2 * h * dk // dv + h
    return pl.pallas_call(
        functools.partial(_retention_kernel, n_chunks=r // RET_CHUNK, chunk=RET_CHUNK, dk=dk),
        grid=(b, h, s // r),
        in_specs=[pl.BlockSpec((None, 8, LANES), lambda i, j, t: (j, 0, 0)),
                  pl.BlockSpec((None, r, dk), lambda i, j, t: (i, t, j)),
                  pl.BlockSpec((None, r, dk), lambda i, j, t: (i, t, k_blk + j)),
                  pl.BlockSpec((None, r, dv), lambda i, j, t: (i, t, v_blk + j)),
                  pl.BlockSpec((None, r, dv), lambda i, j, t: (i, t, g_blk + j)),
                  pl.BlockSpec((None, r, dk), lambda i, j, t: (i, t, 0)),
                  pl.BlockSpec((None, r, dk), lambda i, j, t: (i, t, 0)),
                  pl.BlockSpec((1, dv), lambda i, j, t: (0, j))],
        out_specs=pl.BlockSpec((None, r, dv), lambda i, j, t: (i, t, j)),
        out_shape=jax.ShapeDtypeStruct((b, s, h * dv), BF16),
        scratch_shapes=[pltpu.VMEM((dk, dv), F32)],
        compiler_params=_params("arbitrary", "arbitrary", "arbitrary"),
        name="retention",
    )(lg, proj, proj, proj, proj, cos2, sin2, gain.reshape(1, h * dv))


def _gelu_tanh(x):
    return 0.5 * x * (1.0 + jnp.tanh(0.7978845608028654 * (x + 0.044715 * (x * x * x))))


def _softplus(z):
    return jnp.maximum(z, 0.0) + jnp.log1p(jnp.exp(-jnp.abs(z)))


def _scan_rows(a, u):
    n = a.shape[0]
    row = lax.broadcasted_iota(jnp.int32, a.shape, 0)
    d = 1
    while d < n:
        keep = row >= d
        a_prev = pltpu.roll(a, d, axis=0)
        u_prev = pltpu.roll(u, d, axis=0)
        u = jnp.where(keep, a * u_prev, 0.0) + u
        a = jnp.where(keep, a * a_prev, a)
        d *= 2
    return a, u


def _lru_kernel(x_ref, gate_ref, cw_ref, cb_ref, wcat_ref, ba_ref, bx_ref, lam_ref, gn_ref, o_ref,
                xs_ref, hs_ref, hn_ref, hprev_ref, xtail_ref, *, n_blocks, bw, conv_width, seg, pitch):
    rows = x_ref.shape[0]
    width = x_ref.shape[1]
    halo = 8
    assert rows == 8 * seg and pitch == seg + halo and conv_width - 1 <= halo

    @pl.when(pl.program_id(1) == 0)
    def _():
        xtail_ref[...] = jnp.zeros_like(xtail_ref)
        hprev_ref[...] = jnp.zeros_like(hprev_ref)

    sub = lax.broadcasted_iota(jnp.int32, (8, bw), 0)
    ss = [jnp.zeros((8, 1), F32) for _ in range(seg)]
    for n in range(n_blocks):
        cols = slice(n * bw, (n + 1) * bw)
        slab = xs_ref.at[n]
        xf = x_ref[:, cols].astype(F32)
        slab[0:halo, :] = xtail_ref[:, cols]
        for s in range(8):
            slab[s * pitch + halo:s * pitch + halo + seg, :] = xf[s * seg:(s + 1) * seg, :]
            if s > 0:
                slab[s * pitch:s * pitch + halo, :] = xf[s * seg - halo:s * seg, :]
        xtail_ref[:, cols] = xf[rows - halo:rows, :]
        first = halo - (conv_width - 1)
        taps = [slab[pl.ds(first + o, 8, stride=pitch), :] for o in range(seg + conv_width - 1)]
        cw = [cw_ref[j:j + 1, cols] for j in range(conv_width)]
        cb = cb_ref[:, cols]
        xc_rows = []
        for g in range(seg):
            acc = cb + taps[g] * cw[0]
            for j in range(1, conv_width):
                acc = acc + taps[g + j] * cw[j]
            xc_rows.append(acc)
        xc = jnp.concatenate(xc_rows, axis=0)
        gates = _dot(xc.astype(BF16), wcat_ref[n])
        r = jax.nn.sigmoid(gates[:, :bw] + ba_ref[:, cols])
        i = jax.nn.sigmoid(gates[:, bw:] + bx_ref[:, cols])
        log_a = (-LRU_C) * r * _softplus(-lam_ref[:, cols])
        a = jnp.exp(log_a)
        u = jnp.sqrt(jnp.maximum(1.0 - a * a, 0.0)) * (i * xc)
        h_loc = [u[0:8, :]]
        a_cum = [a[0:8, :]]
        for g in range(1, seg):
            a_g = a[g * 8:(g + 1) * 8, :]
            h_loc.append(a_g * h_loc[-1] + u[g * 8:(g + 1) * 8, :])
            a_cum.append(a_g * a_cum[-1])
        h_in = jnp.broadcast_to(hprev_ref[0:1, cols], (8, bw))
        f_prev = jnp.where(sub == 0, h_in, pltpu.roll(h_loc[-1], 1, axis=0))
        p_prev = jnp.where(sub == 0, 0.0, pltpu.roll(a_cum[-1], 1, axis=0))
        _, carry = _scan_rows(p_prev, f_prev)
        hprev_ref[0:1, cols] = (h_loc[-1] + a_cum[-1] * carry)[7:8, :]
        slab_h = hs_ref.at[n]
        for g in range(seg):
            h_g = h_loc[g] + a_cum[g] * carry
            slab_h[g * 8:(g + 1) * 8, :] = h_g
            ss[g] = ss[g] + jnp.sum(h_g * h_g, axis=-1, keepdims=True)
    inv = [lax.rsqrt(ss_g / width + EPS) for ss_g in ss]
    for n in range(n_blocks):
        cols = slice(n * bw, (n + 1) * bw)
        slab_h = hs_ref.at[n]
        gn = gn_ref[:, cols]
        for g in range(seg):
            hn_ref[pl.ds(g, 8, stride=pitch), :] = slab_h[g * 8:(g + 1) * 8, :] * inv[g] * gn
        for s in range(8):
            t_rows = slice(s * seg, (s + 1) * seg)
            gate = gate_ref[t_rows, cols].astype(F32)
            o_ref[t_rows, cols] = (_gelu_tanh(gate) * hn_ref[s * pitch:s * pitch + seg, :]).astype(o_ref.dtype)


def _rg_lru(proj, gate_off, x_off, conv_w, conv_b, wcat, b_a, b_x, lam, gnorm):
    b, s, _ = proj.shape
    n_blocks, bw, _ = wcat.shape
    w = n_blocks * bw
    assert gate_off % w == 0 and x_off % w == 0
    r = _tile(s, R_LRU)
    seg = r // 8
    pitch = seg + 8
    assert r % 128 == 0 and bw == LANES
    conv_width = conv_w.shape[0]
    vec = pl.BlockSpec((1, w), lambda i, t: (0, 0))
    return pl.pallas_call(
        functools.partial(_lru_kernel, n_blocks=n_blocks, bw=bw, conv_width=conv_width, seg=seg, pitch=pitch),
        grid=(b, s // r),
        in_specs=[pl.BlockSpec((None, r, w), lambda i, t: (i, t, x_off // w)),
                  pl.BlockSpec((None, r, w), lambda i, t: (i, t, gate_off // w)),
                  pl.BlockSpec((conv_width, w), lambda i, t: (0, 0)),
                  vec,
                  pl.BlockSpec((n_blocks, bw, 2 * bw), lambda i, t: (0, 0, 0)),
                  vec, vec, vec, vec],
        out_specs=pl.BlockSpec((None, r, w), lambda i, t: (i, t, 0)),
        out_shape=jax.ShapeDtypeStruct((b, s, w), BF16),
        scratch_shapes=[pltpu.VMEM((n_blocks, 8 * pitch, bw), F32), pltpu.VMEM((n_blocks, r, bw), F32),
                        pltpu.VMEM((8 * pitch, bw), F32), pltpu.VMEM((8, w), F32), pltpu.VMEM((8, w), F32)],
        compiler_params=_params("arbitrary", "arbitrary"),
        name="rg_lru",
    )(proj, proj, conv_w, conv_b.reshape(1, w), wcat, b_a.reshape(1, w), b_x.reshape(1, w),
      lam.reshape(1, w), gnorm.reshape(1, w))


def _dispatch_kernel(pos_ref, pad_lo_ref, pad_hi_ref, nu_ref, src_ref, xs_ref, zero_ref, sem,
                     *, lag, n_tok, n_experts, tm, n_tiles):
    step = pl.program_id(0)

    def row_copy(t, k):
        return pltpu.make_async_copy(src_ref.at[pl.ds(t - step * lag, 1)],
                                     xs_ref.at[pl.ds(pos_ref[k * n_tok + t], 1)], sem.at[0])

    def start(t, carry):
        row_copy(t, 0).start(priority=0)
        row_copy(t, 1).start(priority=1)
        return carry

    def wait(t, carry):
        row_copy(t, 0).wait()
        row_copy(t, 1).wait()
        return carry

    lax.fori_loop(step * lag, (step + 1) * lag, start, 0)

    @pl.when(step == 0)
    def _():
        zero_ref[...] = jnp.zeros_like(zero_ref)

        def pad_copy(p):
            return pltpu.make_async_copy(zero_ref.at[pl.ds(0, 1)], xs_ref.at[pl.ds(p, 1)], sem.at[1])

        def tile_copy(t):
            return pltpu.make_async_copy(zero_ref, xs_ref.at[pl.ds(t * tm, tm)], sem.at[1])

        def run(lo, hi, make):
            def go(p, carry):
                make(p).start()
                return carry

            def done(p, carry):
                make(p).wait()
                return carry

            lax.fori_loop(lo, hi, go, 0)
            lax.fori_loop(lo, hi, done, 0)

        for e in range(n_experts):
            run(pad_lo_ref[e], pad_hi_ref[e], pad_copy)
        run(nu_ref[0], n_tiles, tile_copy)

    lax.fori_loop(step * lag, (step + 1) * lag, wait, 0)


def _dispatch(hn_packed, pos, pad_lo, pad_hi, n_used, tm, n_rows):
    t, n_slab, _ = hn_packed.shape
    n_experts = pad_lo.shape[0]
    lag = _tile(t, DISPATCH_LAG)
    return pl.pallas_call(
        functools.partial(_dispatch_kernel, lag=lag, n_tok=t, n_experts=n_experts, tm=tm, n_tiles=n_rows // tm),
        grid_spec=pltpu.PrefetchScalarGridSpec(
            num_scalar_prefetch=4,
            grid=(t // lag,),
            in_specs=[pl.BlockSpec((lag, n_slab, LANES), lambda i, *_: (i, 0, 0))],
            out_specs=pl.BlockSpec(memory_space=pl.ANY),
            scratch_shapes=[pltpu.VMEM((tm, n_slab, LANES), U32), pltpu.SemaphoreType.DMA((2,))]),
        out_shape=jax.ShapeDtypeStruct((n_rows, n_slab, LANES), U32),
        compiler_params=_params("arbitrary"),
        name="moe_dispatch",
    )(pos, pad_lo, pad_hi, n_used, hn_packed)


def _moe_ffn_kernel(te_ref, nu_ref, xs_ref, wg_ref, wu_ref, wdlo_ref, wdhi_ref, y_ref, abf_ref, h_ref):
    j = pl.program_id(1)
    used = pl.program_id(0) < nu_ref[0]
    tm = abf_ref.shape[0]
    n_up = h_ref.shape[0]
    n_slab = y_ref.shape[0] // tm
    half = n_slab * LANES
    slabs_per_step = wdlo_ref.shape[1] // LANES

    @pl.when(jnp.logical_and(used, j == 0))
    def _():
        for c in range(n_slab):
            lo, hi = _unpack_slab(xs_ref[pl.ds(c, tm, stride=n_slab), :])
            abf_ref[:, c * LANES:(c + 1) * LANES] = lo.astype(BF16)
            abf_ref[:, half + c * LANES:half + (c + 1) * LANES] = hi.astype(BF16)

    @pl.when(jnp.logical_and(used, j < n_up))
    def _():
        a = abf_ref[...]
        g = _dot(a, wg_ref[...])
        u = _dot(a, wu_ref[...])
        h_ref[j] = (_silu(g) * u).astype(BF16)

    @pl.when(jnp.logical_and(used, j >= n_up))
    def _():
        h = jnp.concatenate([h_ref[c] for c in range(n_up)], axis=1)
        lo = pltpu.bitcast(_dot(h, wdlo_ref[...]).astype(BF16).astype(F32), U32) >> 16
        hi = pltpu.bitcast(_dot(h, wdhi_ref[...]).astype(BF16).astype(F32), U32) & jnp.uint32(0xFFFF0000)
        packed = hi | lo
        first = (j - n_up) * slabs_per_step
        for c in range(slabs_per_step):
            y_ref[pl.ds(first + c, tm, stride=n_slab), :] = packed[:, c * LANES:(c + 1) * LANES]

    @pl.when(jnp.logical_and(jnp.logical_not(used), j == 0))
    def _():
        y_ref[...] = jnp.zeros_like(y_ref)


def _moe_ffn(xs, wg_stack, wu_stack, wd_stack, layer_idx, tile_expert, n_used, tm):
    p, n_slab, _ = xs.shape
    d = 2 * n_slab * LANES
    de = wg_stack.shape[3]
    tn = _tile(de, TN_MOE_UP)
    n_up = de // tn
    cw = _tile(d // 2, TN_MOE_DOWN)
    n_down = (d // 2) // cw
    row_in = pl.BlockSpec((tm * n_slab, LANES), lambda i, j, te, nu: (jnp.minimum(i, nu[0] - 1), 0))
    w_up = pl.BlockSpec((None, None, d, tn),
                        lambda i, j, te, nu: (layer_idx, te[i], 0, jnp.minimum(j, n_up - 1)))
    w_down_lo = pl.BlockSpec((None, None, de, cw),
                             lambda i, j, te, nu: (layer_idx, te[i], 0, jnp.maximum(j - n_up, 0)))
    w_down_hi = pl.BlockSpec((None, None, de, cw),
                             lambda i, j, te, nu: (layer_idx, te[i], 0, n_down + jnp.maximum(j - n_up, 0)))
    return pl.pallas_call(
        _moe_ffn_kernel,
        grid_spec=pltpu.PrefetchScalarGridSpec(
            num_scalar_prefetch=2,
            grid=(p // tm, n_up + n_down),
            in_specs=[row_in, w_up, w_up, w_down_lo, w_down_hi],
            out_specs=pl.BlockSpec((tm * n_slab, LANES), lambda i, j, te, nu: (i, 0)),
            scratch_shapes=[pltpu.VMEM((tm, d), BF16), pltpu.VMEM((n_up, tm, tn), BF16)]),
        out_shape=jax.ShapeDtypeStruct((p * n_slab, LANES), U32),
        compiler_params=_params("arbitrary", "arbitrary"),
        name="moe_ffn",
    )(tile_expert, n_used, xs.reshape(p * n_slab, LANES), wg_stack, wu_stack, wd_stack, wd_stack)


def _combine_kernel(pos_ref, y_ref, x_ref, w_ref, mb_ref, tb_ref, o_ref, buf_ref, sem, *, g, n_tok, j_gate):
    step = pl.program_id(0)
    n_slab = buf_ref.shape[2] // g

    def row_copies(base, r, slot):
        dst = pl.multiple_of(r * n_slab, n_slab)
        for k in range(2):
            src = pl.multiple_of(pos_ref[k * n_tok + base + r] * n_slab, n_slab)
            yield pltpu.make_async_copy(y_ref.at[pl.ds(src, n_slab)], buf_ref.at[slot, k, pl.ds(dst, n_slab)],
                                        sem.at[slot])

    def start_row(r, carry):
        for k, cp in enumerate(row_copies(0, r, 0)):
            cp.start(priority=k)
        return carry

    @pl.when(step == 0)
    def _():
        lax.fori_loop(0, g, start_row, 0)

    slot = step % 2

    def wait_row(r, carry):
        for cp in row_copies(step * g, r, slot):
            cp.wait()
        return carry

    lax.fori_loop(0, g, wait_row, 0)

    gate = _mod_rows(mb_ref, tb_ref, j_gate)
    w_a, w_b = w_ref[:, 0:1], w_ref[:, 1:2]
    half = n_slab * LANES
    rows_a, rows_b = buf_ref.at[slot, 0], buf_ref.at[slot, 1]

    def start_next_row(r, carry):
        for k, cp in enumerate(row_copies((step + 1) * g, r, 1 - slot)):
            cp.start(priority=k)
        return carry

    @pl.when(step + 1 < pl.num_programs(0))
    def _():
        lax.fori_loop(0, g, start_next_row, 0)

    for c in range(n_slab):
        a_lo, a_hi = _unpack_slab(rows_a[pl.ds(c, g, stride=n_slab), :])
        b_lo, b_hi = _unpack_slab(rows_b[pl.ds(c, g, stride=n_slab), :])
        lo = slice(c * LANES, (c + 1) * LANES)
        hi = slice(half + c * LANES, half + (c + 1) * LANES)
        o_ref[:, lo] = x_ref[:, lo] + gate[:, lo] * (w_a * a_lo + w_b * b_lo)
        o_ref[:, hi] = x_ref[:, hi] + gate[:, hi] * (w_a * a_hi + w_b * b_hi)


def _moe_combine(y, pos, w2, x, mod_base, table, j_gate, seq):
    t, d = x.shape
    n_slab = d // (2 * LANES)
    g = _tile(seq, G_ROWS)
    return pl.pallas_call(
        functools.partial(_combine_kernel, g=g, n_tok=t, j_gate=j_gate),
        grid_spec=pltpu.PrefetchScalarGridSpec(
            num_scalar_prefetch=1,
            grid=(t // g,),
            in_specs=[pl.BlockSpec(memory_space=pl.ANY),
                      pl.BlockSpec((g, d), lambda i, pos_ref: (i, 0)),
                      pl.BlockSpec((g, 2), lambda i, pos_ref: (i, 0)),
                      pl.BlockSpec((None, N_MOD, d), lambda i, pos_ref: ((i * g) // seq, 0, 0)),
                      pl.BlockSpec((N_MOD, d), lambda i, pos_ref: (0, 0))],
            out_specs=pl.BlockSpec((g, d), lambda i, pos_ref: (i, 0)),
            scratch_shapes=[pltpu.VMEM((2, 2, g * n_slab, LANES), U32), pltpu.SemaphoreType.DMA((2,))]),
        out_shape=jax.ShapeDtypeStruct((t, d), F32),
        compiler_params=_params("arbitrary"),
        name="moe_combine",
    )(pos, y, x, w2, mod_base, table)


def _route_plan(rank, comb, tm):
    t, e = rank.shape
    sel = rank >= 0.0
    counts = jnp.sum(sel.astype(jnp.int32), axis=0)
    padded = ((counts + tm - 1) // tm) * tm
    ends = jnp.cumsum(padded)
    starts = ends - padded
    dest = starts[None, :] + rank.astype(jnp.int32)
    pos_a = jnp.min(jnp.where(sel, dest, jnp.int32(2 ** 30)), axis=1)
    pos_b = jnp.max(jnp.where(sel, dest, jnp.int32(-1)), axis=1)
    w_a = jnp.sum(jnp.where(jnp.logical_and(sel, dest == pos_a[:, None]), comb, 0.0), axis=1)
    w_b = jnp.sum(jnp.where(jnp.logical_and(sel, dest == pos_b[:, None]), comb, 0.0), axis=1)
    pos = jnp.concatenate([pos_a, pos_b]).astype(jnp.int32)
    w2 = jnp.stack([w_a, w_b], axis=1)
    n_rows = 2 * t + e * tm
    n_tiles = n_rows // tm
    tile_start = jnp.arange(n_tiles, dtype=jnp.int32) * tm
    n_used = (ends[-1] // tm).astype(jnp.int32)
    te_raw = jnp.sum((tile_start[:, None] >= ends[None, :]).astype(jnp.int32), axis=1)
    last_e = jnp.sum((ends[-1] - 1 >= ends).astype(jnp.int32))
    tile_expert = jnp.where(tile_start < ends[-1], jnp.minimum(te_raw, e - 1), last_e).astype(jnp.int32)
    return (pos, w2, tile_expert, n_used.reshape(1), (starts + counts).astype(jnp.int32),
            ends.astype(jnp.int32), n_rows)


def kernel(x, c, positions, ada_w, ada_b, ada_table, w_in, conv_w, conv_b, gate_a_w, gate_a_b, gate_x_w,
           gate_x_b, lru_lambda, ret_norm_g, lru_norm_g, w_out, ffn_w_gate, ffn_w_up, ffn_w_down, router_w,
           moe_w_gate, moe_w_up, moe_w_down, final_norm_g):
    b, s, d = x.shape
    depth = w_in.shape[0]
    t = b * s
    ret_v = ret_norm_g.shape[1]
    lru_w = lru_norm_g.shape[1]
    ret_qk = (w_in.shape[2] - 2 * ret_v - 2 * lru_w) // 2
    dk, dv = ret_qk // RET_HEADS, ret_v // RET_HEADS
    gate_off = 2 * ret_qk + 2 * ret_v
    x_off = gate_off + lru_w
    n_experts = router_w.shape[2]
    tm_moe = _tile(t, TM_MOE)

    w_in_b = w_in.astype(BF16)
    w_out_ret = w_out[:, :ret_v].astype(BF16)
    w_out_lru = w_out[:, ret_v:].astype(BF16)
    wcat = jnp.concatenate([gate_a_w, gate_x_w], axis=-1).astype(BF16)
    ffn_g, ffn_u, ffn_d = ffn_w_gate.astype(BF16), ffn_w_up.astype(BF16), ffn_w_down.astype(BF16)
    moe_g, moe_u, moe_d = moe_w_gate.astype(BF16), moe_w_up.astype(BF16), moe_w_down.astype(BF16)

    mod_base = _ada_base(c, ada_w, ada_b)
    cos2, sin2 = _rope_tables(positions, dk)

    for l in range(depth):
        table = ada_table[l]
        hn = _norm_mod(x, mod_base, table, 0)
        proj = _matmul(hn.reshape(t, d), w_in_b, l).reshape(b, s, -1)
        y_ret = _retention(proj, cos2, sin2, ret_norm_g[l], dk, dv)
        y_lru = _rg_lru(proj, gate_off, x_off, conv_w[l], conv_b[l], wcat[l], gate_a_b[l], gate_x_b[l],
                        lru_lambda[l], lru_norm_g[l])
        x = _matmul_residual([y_ret.reshape(t, ret_v), y_lru.reshape(t, lru_w)], [w_out_ret, w_out_lru], l,
                             x.reshape(t, d), mod_base, table, 2, s, TN_OUT, TK_OUT, "out_proj").reshape(b, s, d)
        if l % 2 == 0:
            hn = _norm_mod(x, mod_base, table, 3)
            hmid = _swiglu_up(hn.reshape(t, d), ffn_g, ffn_u, l // 2)
            x = _matmul_residual([hmid], [ffn_d], l // 2, x.reshape(t, d), mod_base, table, 5, s, TN_RES, TK_RES,
                                 "ffn_down").reshape(b, s, d)
        else:
            hn_packed, comb, rank = _router(x, mod_base, table, 3, router_w[l // 2])
            comb = comb.reshape(t, LANES)[:, :n_experts]
            rank = rank.reshape(t, LANES)[:, :n_experts]
            pos, w2, tile_expert, n_used, pad_lo, pad_hi, n_rows = _route_plan(rank, comb, tm_moe)
            xs = _dispatch(hn_packed.reshape(t, -1, LANES), pos, pad_lo, pad_hi, n_used, tm_moe, n_rows)
            y = _moe_ffn(xs, moe_g, moe_u, moe_d, l // 2, tile_expert, n_used, tm_moe)
            x = _moe_combine(y, pos, w2, x.reshape(t, d), mod_base, table, 5, s).reshape(b, s, d)
    return _final_norm(x, final_norm_g)
```
